```python
import math
import jax, jax.numpy as jnp
from jax import lax
import numpy as np

D_MODEL = 1024
BATCH = 32
SEQ = 2048
DEPTH = 2
DEC_BATCH = 8
DEC_SEQ = 8192
PAST_LEN = 128

N_HEADS = 8
HEAD_DIM = 64
V_DIM = 2 * HEAD_DIM
ATT_W = N_HEADS * 2 * HEAD_DIM
Q_BLOCK = 128
ROPE_THETA = 10000.0
LRU_W = 1024
N_LRU_BLOCKS = 8
LRU_BLOCK = LRU_W // N_LRU_BLOCKS
LRU_CONV = 4
LRU_CONV_LEFT = 2
LRU_C = 8.0
D_FF = 3072
FFN_CONV = 3
FFN_CONV_LEFT = 1
PLE_DIM = 256
EPS = 1e-6
IN_COLS = 3 * ATT_W + 2 * LRU_W + 2 * D_MODEL
SPLITS = (ATT_W, 2 * ATT_W, 3 * ATT_W, 3 * ATT_W + LRU_W, 3 * ATT_W + 2 * LRU_W,
          3 * ATT_W + 2 * LRU_W + D_MODEL)

kernel_name = "hybrid_diffattn_rglru_encoder"


def rms_norm(x, g):
    xf = x.astype(jnp.float32)
    y = xf * lax.rsqrt(jnp.mean(xf * xf, axis=-1, keepdims=True) + EPS)
    return (y * g.astype(jnp.float32)).astype(x.dtype)


def dwconv(x, w, b, left):
    K = w.shape[0]
    S = x.shape[1]
    xp = jnp.pad(x, ((0, 0), (left, K - 1 - left), (0, 0)))
    return sum(xp[:, k:k + S] * w[k] for k in range(K)) + b


def rope(x):
    S = x.shape[1]
    inv = ROPE_THETA ** (-jnp.arange(0, HEAD_DIM, 2, dtype=jnp.float32) / HEAD_DIM)
    ang = jnp.arange(S, dtype=jnp.float32)[:, None] * inv[None, :]
    cos = jnp.cos(ang)[:, None, None, :]
    sin = jnp.sin(ang)[:, None, None, :]
    xf = x.astype(jnp.float32)
    x1, x2 = xf[..., :HEAD_DIM // 2], xf[..., HEAD_DIM // 2:]
    out = jnp.concatenate([x1 * cos - x2 * sin, x2 * cos + x1 * sin], axis=-1)
    return out.astype(x.dtype)


def diff_attention(q, k, v, lam):
    B, S = q.shape[0], q.shape[1]
    nb = S // Q_BLOCK
    qt = q.transpose(0, 2, 3, 1, 4)
    kt = k.transpose(0, 2, 3, 1, 4)
    vt = v.transpose(0, 2, 1, 3)
    qb = qt.reshape(B, N_HEADS, 2, nb, Q_BLOCK, HEAD_DIM).transpose(3, 0, 1, 2, 4, 5)
    scale = HEAD_DIM ** -0.5

    def block(qi):
        s = jnp.einsum('bhcqd,bhckd->bhcqk', qi, kt).astype(jnp.float32) * scale
        pr = jax.nn.softmax(s, axis=-1)
        w = pr[:, :, 0] - lam * pr[:, :, 1]
        return jnp.einsum('bhqk,bhkv->bhqv', w.astype(vt.dtype), vt)

    o = lax.map(block, qb)
    return o.transpose(1, 0, 3, 2, 4).reshape(B, S, N_HEADS, V_DIM)


def linear_scan(a, b, reverse):
    def comb(l, r):
        return (l[0] * r[0], r[0] * l[1] + r[1])
    _, h = lax.associative_scan(comb, (a, b), reverse=reverse, axis=1)
    return h


def rg_lru_direction(x, wa, ba, wx, bx, lam, reverse):
    B, S, _ = x.shape
    xb = x.reshape(B, S, N_LRU_BLOCKS, LRU_BLOCK)
    r = jax.nn.sigmoid(jnp.einsum('bsni,nij->bsnj', xb, wa.astype(jnp.float32)).reshape(B, S, LRU_W)
                       + ba.astype(jnp.float32))
    i = jax.nn.sigmoid(jnp.einsum('bsni,nij->bsnj', xb, wx.astype(jnp.float32)).reshape(B, S, LRU_W)
                       + bx.astype(jnp.float32))
    log_a = -LRU_C * r * jax.nn.softplus(-lam.astype(jnp.float32))
    a = jnp.exp(log_a)
    b = jnp.sqrt(-jnp.expm1(2.0 * log_a)) * (i * x)
    return linear_scan(a, b, reverse)


def rg_lru_bidir(x, wa, ba, wx, bx, lam):
    xf = x.astype(jnp.float32)
    h = (rg_lru_direction(xf, wa[0], ba[0], wx[0], bx[0], lam[0], False)
         + rg_lru_direction(xf, wa[1], ba[1], wx[1], bx[1], lam[1], True))
    return h.astype(x.dtype)


def layer(x, p, lp, lam_init):
    (norm_mix_pre, norm_mix_post, w_in, lam_q1, lam_k1, lam_q2, lam_k2, subln_g,
     lru_conv_w, lru_conv_b, rg_wa, rg_ba, rg_wx, rg_bx, rg_lambda,
     w_branch_attn, w_branch_lru, w_out, norm_ffn_pre, norm_ffn_post,
     w_ffn_up, ffn_conv_w, ffn_conv_b, w_ffn_down, w_ple_proj, w_ple_gate) = lp
    B, S, _ = x.shape
    h = rms_norm(x, norm_mix_pre)
    z = h @ w_in
    q, k, v, u, gl, ga, gb = jnp.split(z, SPLITS, axis=-1)
    q = rope(q.reshape(B, S, N_HEADS, 2, HEAD_DIM))
    k = rope(k.reshape(B, S, N_HEADS, 2, HEAD_DIM))
    v = v.reshape(B, S, N_HEADS, V_DIM)
    lam = (jnp.exp(jnp.sum(lam_q1.astype(jnp.float32) * lam_k1.astype(jnp.float32)))
           - jnp.exp(jnp.sum(lam_q2.astype(jnp.float32) * lam_k2.astype(jnp.float32)))
           + lam_init)
    o = diff_attention(q, k, v, lam)
    o_attn = (rms_norm(o, subln_g) * (1.0 - lam_init)).reshape(B, S, ATT_W)
    u = dwconv(u, lru_conv_w, lru_conv_b, LRU_CONV_LEFT)
    o_lru = jax.nn.gelu(gl) * rg_lru_bidir(u, rg_wa, rg_ba, rg_wx, rg_bx, rg_lambda)
    m = jax.nn.sigmoid(ga) * (o_attn @ w_branch_attn) + jax.nn.sigmoid(gb) * (o_lru @ w_branch_lru)
    x = x + rms_norm(m @ w_out, norm_mix_post)
    h = rms_norm(x, norm_ffn_pre)
    c, lin = jnp.split(h @ w_ffn_up, 2, axis=-1)
    c = dwconv(c, ffn_conv_w, ffn_conv_b, FFN_CONV_LEFT)
    f = (jax.nn.gelu(c) * lin) @ w_ffn_down
    x = x + rms_norm(f, norm_ffn_post)
    x = x + jax.nn.sigmoid(x @ w_ple_gate) * (p @ w_ple_proj)
    return x


def setup_inputs(seed: int = 0) -> dict:
    key = jax.random.key(seed)
    ks = jax.random.split(key, 40)
    f32 = jnp.float32
    L = DEPTH

    def nrm(k, shape, scale):
        return jax.random.normal(k, shape, f32) * scale

    def gain(k, shape):
        return 1.0 + 0.05 * jax.random.normal(k, shape, f32)

    a0 = jax.random.uniform(ks[14], (L, 2, LRU_W), f32, 0.9, 0.999)
    s0 = a0 ** (1.0 / LRU_C)
    rg_lambda = jnp.log(s0) - jnp.log1p(-s0)
    return {
        "x_prompt": nrm(ks[0], (BATCH, SEQ, D_MODEL), 1.0),
        "x_sample": nrm(ks[1], (DEC_BATCH, DEC_SEQ, D_MODEL), 1.0),
        "p_prompt": nrm(ks[2], (DEPTH, BATCH, SEQ, PLE_DIM), 1.0),
        "p_sample": nrm(ks[3], (DEPTH, DEC_BATCH, DEC_SEQ, PLE_DIM), 1.0),
        "norm_mix_pre": gain(ks[4], (L, D_MODEL)),
        "norm_mix_post": gain(ks[5], (L, D_MODEL)),
        "w_in": nrm(ks[6], (L, D_MODEL, IN_COLS), D_MODEL ** -0.5),
        "lam_q1": nrm(ks[7], (L, HEAD_DIM), 0.1),
        "lam_k1": nrm(ks[8], (L, HEAD_DIM), 0.1),
        "lam_q2": nrm(ks[9], (L, HEAD_DIM), 0.1),
        "lam_k2": nrm(ks[10], (L, HEAD_DIM), 0.1),
        "subln_g": gain(ks[11], (L, V_DIM)),
        "lru_conv_w": nrm(ks[12], (L, LRU_CONV, LRU_W), LRU_CONV ** -0.5),
        "lru_conv_b": nrm(ks[13], (L, LRU_W), 0.02),
        "rg_wa": nrm(ks[15], (L, 2, N_LRU_BLOCKS, LRU_BLOCK, LRU_BLOCK), LRU_BLOCK ** -0.5),
        "rg_ba": nrm(ks[16], (L, 2, LRU_W), 0.02),
        "rg_wx": nrm(ks[17], (L, 2, N_LRU_BLOCKS, LRU_BLOCK, LRU_BLOCK), LRU_BLOCK ** -0.5),
        "rg_bx": nrm(ks[18], (L, 2, LRU_W), 0.02),
        "rg_lambda": rg_lambda,
        "w_branch_attn": nrm(ks[19], (L, ATT_W, D_MODEL), ATT_W ** -0.5),
        "w_branch_lru": nrm(ks[20], (L, LRU_W, D_MODEL), LRU_W ** -0.5),
        "w_out": nrm(ks[21], (L, D_MODEL, D_MODEL), D_MODEL ** -0.5),
        "norm_ffn_pre": gain(ks[22], (L, D_MODEL)),
        "norm_ffn_post": gain(ks[23], (L, D_MODEL)),
        "w_ffn_up": nrm(ks[24], (L, D_MODEL, 2 * D_FF), D_MODEL ** -0.5),
        "ffn_conv_w": nrm(ks[25], (L, FFN_CONV, D_FF), FFN_CONV ** -0.5),
        "ffn_conv_b": nrm(ks[26], (L, D_FF), 0.02),
        "w_ffn_down": nrm(ks[27], (L, D_FF, D_MODEL), D_FF ** -0.5),
        "w_ple_proj": nrm(ks[28], (L, PLE_DIM, D_MODEL), PLE_DIM ** -0.5),
        "w_ple_gate": nrm(ks[29], (L, D_MODEL, D_MODEL), D_MODEL ** -0.5),
    }


def reference(x_prompt, x_sample, p_prompt, p_sample, norm_mix_pre, norm_mix_post, w_in,
              lam_q1, lam_k1, lam_q2, lam_k2, subln_g, lru_conv_w, lru_conv_b,
              rg_wa, rg_ba, rg_wx, rg_bx, rg_lambda, w_branch_attn, w_branch_lru, w_out,
              norm_ffn_pre, norm_ffn_post, w_ffn_up, ffn_conv_w, ffn_conv_b, w_ffn_down,
              w_ple_proj, w_ple_gate):
    xp = x_prompt
    xs = x_sample
    for i in range(DEPTH):
        lp = (norm_mix_pre[i], norm_mix_post[i], w_in[i], lam_q1[i], lam_k1[i], lam_q2[i],
              lam_k2[i], subln_g[i], lru_conv_w[i], lru_conv_b[i], rg_wa[i], rg_ba[i],
              rg_wx[i], rg_bx[i], rg_lambda[i], w_branch_attn[i], w_branch_lru[i], w_out[i],
              norm_ffn_pre[i], norm_ffn_post[i], w_ffn_up[i], ffn_conv_w[i], ffn_conv_b[i],
              w_ffn_down[i], w_ple_proj[i], w_ple_gate[i])
        lam_init = 0.8 - 0.6 * math.exp(-0.3 * i)
        xp = layer(xp, p_prompt[i], lp, lam_init)
        xs = layer(xs, p_sample[i], lp, lam_init)
    return (xp, xs)
```

```python
import functools
import math

import jax
import jax.numpy as jnp
from jax import lax
from jax.experimental import pallas as pl
from jax.experimental.pallas import tpu as pltpu

F32 = jnp.float32
BF16 = jnp.bfloat16

D_MODEL = 1024
N_HEADS = 8
HEAD_DIM = 64
V_DIM = 2 * HEAD_DIM
ATT_W = N_HEADS * V_DIM
LRU_W = 1024
N_LRU_BLOCKS = 8
LRU_BLOCK = LRU_W // N_LRU_BLOCKS
LRU_CONV_LEFT = 2
LRU_C = 8.0
D_FF = 3072
PLE_DIM = 256
EPS = 1e-6
ROPE_THETA = 10000.0
IN_COLS = 3 * ATT_W + 2 * LRU_W + 2 * D_MODEL

LANES = 128
SUBLANES = 8
BF16_ROWS = 16
VMEM_LIMIT_BYTES = 56 * 1024 * 1024

COL_Q, COL_K, COL_V, COL_U, COL_GL, COL_GA, COL_GB = range(7)


def _cparams(*sem):
    return pltpu.CompilerParams(dimension_semantics=sem, vmem_limit_bytes=VMEM_LIMIT_BYTES)


def _rms(x, g):
    return x * lax.rsqrt(jnp.mean(x * x, axis=-1, keepdims=True) + EPS) * g


def _norm_matmul_kernel(*refs, n_rope_tiles, tn):
    if n_rope_tiles:
        x_ref, g_ref, w_ref, cos_ref, sa_ref, sb_ref, o_ref, h_scr = refs
    else:
        x_ref, g_ref, w_ref, o_ref, h_scr = refs
    j = pl.program_id(1)

    @pl.when(j == 0)
    def _():
        h_scr[...] = _rms(x_ref[...], g_ref[...]).astype(BF16)

    z = jnp.dot(h_scr[...], w_ref[...], preferred_element_type=F32)
    if n_rope_tiles:
        @pl.when(j < n_rope_tiles)
        def _():
            cos, sa, sb = cos_ref[...], sa_ref[...], sb_ref[...]
            for g in range(tn // LANES):
                zg = z[:, g * LANES:(g + 1) * LANES]
                r = (zg * cos + pltpu.roll(zg, LANES - HEAD_DIM // 2, 1) * sa
                     + pltpu.roll(zg, HEAD_DIM // 2, 1) * sb)
                o_ref[:, g * LANES:(g + 1) * LANES] = r.astype(o_ref.dtype)

        @pl.when(j >= n_rope_tiles)
        def _():
            o_ref[...] = z.astype(o_ref.dtype)
    else:
        o_ref[...] = z.astype(o_ref.dtype)


def _norm_matmul(x, g, w, rope, *, seq, tm, tn):
    T, K = x.shape
    N = w.shape[1]
    tm = min(tm, seq)
    n_rope_tiles = (2 * ATT_W) // tn if rope is not None else 0
    in_specs = [
        pl.BlockSpec((tm, K), lambda i, j: (i, 0)),
        pl.BlockSpec((1, K), lambda i, j: (0, 0)),
        pl.BlockSpec((K, tn), lambda i, j: (0, j)),
    ]
    args = [x, g, w]
    if rope is not None:
        nseq = seq // tm
        tab_spec = pl.BlockSpec((tm, LANES), lambda i, j: (i % nseq, 0))
        in_specs += [tab_spec] * 3
        args += list(rope)
    return pl.pallas_call(
        functools.partial(_norm_matmul_kernel, n_rope_tiles=n_rope_tiles, tn=tn),
        grid=(T // tm, N // tn),
        in_specs=in_specs,
        out_specs=pl.BlockSpec((tm, tn), lambda i, j: (i, j)),
        out_shape=jax.ShapeDtypeStruct((T, N), BF16),
        scratch_shapes=[pltpu.VMEM((tm, K), BF16)],
        compiler_params=_cparams("parallel", "arbitrary"),
        name="norm_matmul_rope" if rope is not None else "norm_matmul",
    )(*args)


def _attn_kernel(lamp_ref, g_ref, q_ref, k_ref, v_ref, o_ref, m_scr, l_scr, acc_scr, *, lam_init, nk):
    ki = pl.program_id(3)

    @pl.when(ki == 0)
    def _():
        m_scr[...] = jnp.full(m_scr.shape, -jnp.inf, F32)
        l_scr[...] = jnp.zeros(l_scr.shape, F32)
        acc_scr[...] = jnp.zeros(acc_scr.shape, F32)

    q = q_ref[...]
    k = k_ref[...]
    v = v_ref[...]
    first_map = lax.broadcasted_iota(jnp.int32, q.shape, 1) < HEAD_DIM
    zero = jnp.zeros_like(q)
    for c in range(2):
        qc = jnp.where(first_map, q, zero) if c == 0 else jnp.where(first_map, zero, q)
        s = lax.dot_general(qc, k, (((1,), (1,)), ((), ())), preferred_element_type=F32)
        m_prev = m_scr[c]
        m_new = jnp.maximum(m_prev, jnp.max(s, axis=-1, keepdims=True))
        alpha = jnp.exp(m_prev - m_new)
        e = jnp.exp(s - m_new)
        l_scr[c] = alpha * l_scr[c] + jnp.sum(e, axis=-1, keepdims=True)
        acc_scr[c] = alpha * acc_scr[c] + jnp.dot(e.astype(BF16), v, preferred_element_type=F32)
        m_scr[c] = m_new

    @pl.when(ki == nk - 1)
    def _():
        lp = lamp_ref[...]
        lam = (jnp.exp(jnp.sum(lp[0:1] * lp[1:2], axis=-1, keepdims=True))
               - jnp.exp(jnp.sum(lp[2:3] * lp[3:4], axis=-1, keepdims=True)) + lam_init)
        o = acc_scr[0] / l_scr[0] - lam * (acc_scr[1] / l_scr[1])
        o_ref[...] = (_rms(o, g_ref[...]) * (1.0 - lam_init)).astype(o_ref.dtype)


def _attention(z, lamp, subln_g, *, batch, seq, lam_init, tq, tk):
    T = z.shape[0]
    tq, tk = min(tq, seq), min(tk, seq)
    nq, nk = seq // tq, seq // tk
    return pl.pallas_call(
        functools.partial(_attn_kernel, lam_init=lam_init, nk=nk),
        grid=(batch, N_HEADS, nq, nk),
        in_specs=[
            pl.BlockSpec((4, HEAD_DIM), lambda b, h, qi, ki: (0, 0)),
            pl.BlockSpec((1, V_DIM), lambda b, h, qi, ki: (0, 0)),
            pl.BlockSpec((tq, V_DIM), lambda b, h, qi, ki: (b * nq + qi, COL_Q * N_HEADS + h)),
            pl.BlockSpec((tk, V_DIM), lambda b, h, qi, ki: (b * nk + ki, COL_K * N_HEADS + h)),
            pl.BlockSpec((tk, V_DIM), lambda b, h, qi, ki: (b * nk + ki, COL_V * N_HEADS + h)),
        ],
        out_specs=pl.BlockSpec((tq, V_DIM), lambda b, h, qi, ki: (b * nq + qi, h)),
        out_shape=jax.ShapeDtypeStruct((T, ATT_W), BF16),
        scratch_shapes=[pltpu.VMEM((2, tq, 1), F32), pltpu.VMEM((2, tq, 1), F32),
                        pltpu.VMEM((2, tq, V_DIM), F32)],
        compiler_params=_cparams("parallel", "parallel", "parallel", "arbitrary"),
        name="diff_attention",
    )(lamp, subln_g, z, z, z)


def _softplus(y):
    return jnp.maximum(y, 0.0) + jnp.log1p(jnp.exp(-jnp.abs(y)))


def _lru_kernel(*refs, reverse, ts, ns):
    if reverse:
        (u_ref, up_ref, un_ref, cw_ref, cb_ref, wri_ref, bri_ref, lam_ref, gl_ref, hf_ref,
         o_ref, a_scr, b_scr, h_scr) = refs
    else:
        (u_ref, up_ref, un_ref, cw_ref, cb_ref, wri_ref, bri_ref, lam_ref,
         o_ref, a_scr, b_scr, h_scr) = refs
    j = pl.program_id(1)
    jj = (ns - 1 - j) if reverse else j

    @pl.when(j == 0)
    def _():
        h_scr[...] = jnp.zeros(h_scr.shape, F32)

    u = u_ref[...].astype(F32)
    up = jnp.where(jj > 0, up_ref[...].astype(F32), 0.0)
    un = jnp.where(jj < ns - 1, un_ref[...].astype(F32), 0.0)
    row = lax.broadcasted_iota(jnp.int32, u.shape, 0)
    cw = cw_ref[...]
    xc = cb_ref[...] + u * cw[LRU_CONV_LEFT:LRU_CONV_LEFT + 1]
    for tap in range(cw.shape[0]):
        off = tap - LRU_CONV_LEFT
        if off == 0:
            continue
        sh = pltpu.roll(u, (-off) % ts, 0)
        for e in range(abs(off)):
            if off < 0:
                sh = jnp.where(row == e, up[BF16_ROWS + off + e:BF16_ROWS + off + e + 1], sh)
            else:
                sh = jnp.where(row == ts - off + e, un[e:e + 1], sh)
        xc = xc + sh * cw[tap:tap + 1]
    xcb = xc.astype(BF16)
    neg_c_softplus = -LRU_C * _softplus(-lam_ref[...])
    for n in range(N_LRU_BLOCKS):
        sl = slice(n * LRU_BLOCK, (n + 1) * LRU_BLOCK)
        pre = jnp.dot(xcb[:, sl], wri_ref[n], preferred_element_type=F32) + bri_ref[n]
        r = jax.nn.sigmoid(pre[:, :LRU_BLOCK])
        i = jax.nn.sigmoid(pre[:, LRU_BLOCK:])
        log_a = r * neg_c_softplus[:, sl]
        a = jnp.exp(log_a)
        a_scr[:, sl] = a
        b_scr[:, sl] = jnp.sqrt(-jnp.tanh(log_a) * (a * a + 1.0)) * (i * xc[:, sl])

    rows = lax.broadcasted_iota(jnp.int32, (SUBLANES, LRU_W), 0)
    ng = ts // SUBLANES

    def group(g, h):
        gi = (ng - 1 - g) if reverse else g
        r0 = pl.multiple_of(gi * SUBLANES, SUBLANES)
        a8 = a_scr[pl.ds(r0, SUBLANES), :]
        b8 = b_scr[pl.ds(r0, SUBLANES), :]
        for d in (1, 2, 4):
            shift = (SUBLANES - d) if reverse else d
            keep = (rows < SUBLANES - d) if reverse else (rows >= d)
            a_sh = jnp.where(keep, pltpu.roll(a8, shift, 0), 1.0)
            b_sh = jnp.where(keep, pltpu.roll(b8, shift, 0), 0.0)
            b8 = a8 * b_sh + b8
            a8 = a8 * a_sh
        h8 = a8 * h + b8
        b_scr[pl.ds(r0, SUBLANES), :] = h8
        return h8[0:1, :] if reverse else h8[SUBLANES - 1:SUBLANES, :]

    h_scr[...] = lax.fori_loop(0, ng, group, h_scr[...])
    if reverse:
        hsum = hf_ref[...] + b_scr[...]
        o_ref[...] = (jax.nn.gelu(gl_ref[...].astype(F32)) * hsum).astype(o_ref.dtype)
    else:
        o_ref[...] = b_scr[...]


def _lru(z, conv_w, conv_b, wri, bri, lam, h_fwd, *, batch, seq, ts):
    T = z.shape[0]
    ts = min(ts, seq)
    ns = seq // ts
    reverse = h_fwd is not None
    halo_per_chunk = ts // BF16_ROWS
    last_halo = T // BF16_ROWS - 1

    def chunk(j):
        return (ns - 1 - j) if reverse else j

    def row_block(b, j):
        return b * ns + chunk(j)

    def prev_halo(b, j):
        return (jnp.maximum(row_block(b, j) * halo_per_chunk - 1, 0), COL_U)

    def next_halo(b, j):
        return (jnp.minimum((row_block(b, j) + 1) * halo_per_chunk, last_halo), COL_U)

    const = lambda b, j: (0, 0)
    in_specs = [
        pl.BlockSpec((ts, LRU_W), lambda b, j: (row_block(b, j), COL_U)),
        pl.BlockSpec((BF16_ROWS, LRU_W), prev_halo),
        pl.BlockSpec((BF16_ROWS, LRU_W), next_halo),
        pl.BlockSpec(conv_w.shape, const),
        pl.BlockSpec((1, LRU_W), const),
        pl.BlockSpec(wri.shape, lambda b, j: (0, 0, 0)),
        pl.BlockSpec(bri.shape, lambda b, j: (0, 0, 0)),
        pl.BlockSpec((1, LRU_W), const),
    ]
    args = [z, z, z, conv_w, conv_b, wri, bri, lam]
    if reverse:
        in_specs += [
            pl.BlockSpec((ts, LRU_W), lambda b, j: (row_block(b, j), COL_GL)),
            pl.BlockSpec((ts, LRU_W), lambda b, j: (row_block(b, j), 0)),
        ]
        args += [z, h_fwd]
    return pl.pallas_call(
        functools.partial(_lru_kernel, reverse=reverse, ts=ts, ns=ns),
        grid=(batch, ns),
        in_specs=in_specs,
        out_specs=pl.BlockSpec((ts, LRU_W), lambda b, j: (row_block(b, j), 0)),
        out_shape=jax.ShapeDtypeStruct((T, LRU_W), BF16 if reverse else F32),
        scratch_shapes=[pltpu.VMEM((ts, LRU_W), F32), pltpu.VMEM((ts, LRU_W), F32),
                        pltpu.VMEM((1, LRU_W), F32)],
        compiler_params=_cparams("parallel", "arbitrary"),
        name="lru_bwd" if reverse else "lru_fwd",
    )(*args)


def _merge_kernel(oa_ref, ol_ref, ga_ref, gb_ref, x_ref, wba_ref, wbl_ref, wo_ref, g_ref, o_ref):
    pa = jnp.dot(oa_ref[...], wba_ref[...], preferred_element_type=F32)
    pb = jnp.dot(ol_ref[...], wbl_ref[...], preferred_element_type=F32)
    m = (jax.nn.sigmoid(ga_ref[...].astype(F32)) * pa
         + jax.nn.sigmoid(gb_ref[...].astype(F32)) * pb)
    y = jnp.dot(m.astype(BF16), wo_ref[...], preferred_element_type=F32)
    o_ref[...] = x_ref[...] + _rms(y, g_ref[...])


def _merge(o_attn, o_lru, z, x, w_ba, w_bl, w_out, g, *, seq, tm):
    T = x.shape[0]
    tm = min(tm, seq)
    row = lambda i: (i, 0)
    const = lambda i: (0, 0)
    wspec = pl.BlockSpec((D_MODEL, D_MODEL), const)
    return pl.pallas_call(
        _merge_kernel,
        grid=(T // tm,),
        in_specs=[
            pl.BlockSpec((tm, ATT_W), row),
            pl.BlockSpec((tm, LRU_W), row),
            pl.BlockSpec((tm, D_MODEL), lambda i: (i, COL_GA)),
            pl.BlockSpec((tm, D_MODEL), lambda i: (i, COL_GB)),
            pl.BlockSpec((tm, D_MODEL), row),
            wspec, wspec, wspec,
            pl.BlockSpec((1, D_MODEL), const),
        ],
        out_specs=pl.BlockSpec((tm, D_MODEL), row),
        out_shape=jax.ShapeDtypeStruct((T, D_MODEL), F32),
        compiler_params=_cparams("parallel"),
        name="merge_out",
    )(o_attn, o_lru, z, z, x, w_ba, w_bl, w_out, g)


def _ffn_down_kernel(c_ref, cp_ref, cn_ref, lin_ref, x_ref, p_ref, cw_ref, cb_ref, wd_ref, g_ref,
                     wg_ref, wp_ref, o_ref, *, tm, nseq):
    pos = pl.program_id(0) % nseq
    c = c_ref[...].astype(F32)
    cp = jnp.where(pos > 0, cp_ref[BF16_ROWS - 1:BF16_ROWS, :].astype(F32), 0.0)
    cn = jnp.where(pos < nseq - 1, cn_ref[0:1, :].astype(F32), 0.0)
    row = lax.broadcasted_iota(jnp.int32, c.shape, 0)
    c_prev = jnp.where(row == 0, cp, pltpu.roll(c, 1, 0))
    c_next = jnp.where(row == tm - 1, cn, pltpu.roll(c, tm - 1, 0))
    cw = cw_ref[...]
    conv = c_prev * cw[0:1] + c * cw[1:2] + c_next * cw[2:3] + cb_ref[...]
    act = (jax.nn.gelu(conv) * lin_ref[...].astype(F32)).astype(BF16)
    f = jnp.dot(act, wd_ref[...], preferred_element_type=F32)
    x = x_ref[...] + _rms(f, g_ref[...])
    gate = jax.nn.sigmoid(jnp.dot(x.astype(BF16), wg_ref[...], preferred_element_type=F32))
    emb = jnp.dot(p_ref[...].astype(BF16), wp_ref[...], preferred_element_type=F32)
    o_ref[...] = x + gate * emb


def _ffn_down(cl, x, p, conv_w, conv_b, w_down, g, w_gate, w_proj, *, seq, tm):
    T = x.shape[0]
    tm = min(tm, seq)
    nseq = seq // tm
    halo_per_tile = tm // BF16_ROWS
    last_halo = T // BF16_ROWS - 1
    row = lambda i: (i, 0)
    const = lambda i: (0, 0)
    return pl.pallas_call(
        functools.partial(_ffn_down_kernel, tm=tm, nseq=nseq),
        grid=(T // tm,),
        in_specs=[
            pl.BlockSpec((tm, D_FF), row),
            pl.BlockSpec((BF16_ROWS, D_FF), lambda i: (jnp.maximum(i * halo_per_tile - 1, 0), 0)),
            pl.BlockSpec((BF16_ROWS, D_FF),
                         lambda i: (jnp.minimum((i + 1) * halo_per_tile, last_halo), 0)),
            pl.BlockSpec((tm, D_FF), lambda i: (i, 1)),
            pl.BlockSpec((tm, D_MODEL), row),
            pl.BlockSpec((tm, PLE_DIM), row),
            pl.BlockSpec(conv_w.shape, const),
            pl.BlockSpec((1, D_FF), const),
            pl.BlockSpec((D_FF, D_MODEL), const),
            pl.BlockSpec((1, D_MODEL), const),
            pl.BlockSpec((D_MODEL, D_MODEL), const),
            pl.BlockSpec((PLE_DIM, D_MODEL), const),
        ],
        out_specs=pl.BlockSpec((tm, D_MODEL), row),
        out_shape=jax.ShapeDtypeStruct((T, D_MODEL), F32),
        compiler_params=_cparams("parallel"),
        name="ffn_down_ple",
    )(cl, cl, cl, cl, x, p, conv_w, conv_b, w_down, g, w_gate, w_proj)


def _rope_tables(seq):
    half = HEAD_DIM // 2
    inv = ROPE_THETA ** (-jnp.arange(0, HEAD_DIM, 2, dtype=F32) / HEAD_DIM)
    ang = jnp.arange(seq, dtype=F32)[:, None] * inv[None, :]
    lane = jnp.arange(LANES)
    cos = jnp.cos(ang)[:, lane % half]
    sin = jnp.sin(ang)[:, lane % half]
    lower = (lane % HEAD_DIM) < half
    return cos, jnp.where(lower, -sin, 0.0), jnp.where(lower, 0.0, sin)


def _prep_layer(i, w):
    q_scale = jnp.where(jnp.arange(IN_COLS) < ATT_W, HEAD_DIM ** -0.5, 1.0).astype(F32)
    row = lambda a: a.reshape(1, -1).astype(F32)
    wri = jnp.concatenate([w["rg_wa"][i], w["rg_wx"][i]], axis=-1).astype(BF16)
    bri = jnp.concatenate([w["rg_ba"][i].reshape(2, N_LRU_BLOCKS, 1, LRU_BLOCK),
                           w["rg_bx"][i].reshape(2, N_LRU_BLOCKS, 1, LRU_BLOCK)], axis=-1)
    return dict(
        norm_mix_pre=row(w["norm_mix_pre"][i]),
        norm_mix_post=row(w["norm_mix_post"][i]),
        w_in=(w["w_in"][i] * q_scale).astype(BF16),
        lamp=jnp.stack([w["lam_q1"][i], w["lam_k1"][i], w["lam_q2"][i], w["lam_k2"][i]]).astype(F32),
        subln_g=row(w["subln_g"][i]),
        lru_conv_w=w["lru_conv_w"][i].astype(F32),
        lru_conv_b=row(w["lru_conv_b"][i]),
        wri=wri, bri=bri.astype(F32),
        rg_lambda=w["rg_lambda"][i].reshape(2, 1, LRU_W).astype(F32),
        w_branch_attn=w["w_branch_attn"][i].astype(BF16),
        w_branch_lru=w["w_branch_lru"][i].astype(BF16),
        w_out=w["w_out"][i].astype(BF16),
        norm_ffn_pre=row(w["norm_ffn_pre"][i]),
        norm_ffn_post=row(w["norm_ffn_post"][i]),
        w_ffn_up=w["w_ffn_up"][i].astype(BF16),
        ffn_conv_w=w["ffn_conv_w"][i].astype(F32),
        ffn_conv_b=row(w["ffn_conv_b"][i]),
        w_ffn_down=w["w_ffn_down"][i].astype(BF16),
        w_ple_proj=w["w_ple_proj"][i].astype(BF16),
        w_ple_gate=w["w_ple_gate"][i].astype(BF16),
    )


def _layer(x, p, lp, rope, lam_init, *, batch, seq, tiles):
    z = _norm_matmul(x, lp["norm_mix_pre"], lp["w_in"], rope, seq=seq, tm=tiles["tm"], tn=tiles["tn"])
    o_attn = _attention(z, lp["lamp"], lp["subln_g"], batch=batch, seq=seq, lam_init=lam_init,
                        tq=tiles["tq"], tk=tiles["tk"])
    h_fwd = _lru(z, lp["lru_conv_w"], lp["lru_conv_b"], lp["wri"][0], lp["bri"][0], lp["rg_lambda"][0],
                 None, batch=batch, seq=seq, ts=tiles["ts"])
    o_lru = _lru(z, lp["lru_conv_w"], lp["lru_conv_b"], lp["wri"][1], lp["bri"][1], lp["rg_lambda"][1],
                 h_fwd, batch=batch, seq=seq, ts=tiles["ts"])
    x1 = _merge(o_attn, o_lru, z, x, lp["w_branch_attn"], lp["w_branch_lru"], lp["w_out"],
                lp["norm_mix_post"], seq=seq, tm=tiles["tm"])
    cl = _norm_matmul(x1, lp["norm_ffn_pre"], lp["w_ffn_up"], None, seq=seq, tm=tiles["tm"], tn=tiles["tn"])
    return _ffn_down(cl, x1, p, lp["ffn_conv_w"], lp["ffn_conv_b"], lp["w_ffn_down"], lp["norm_ffn_post"],
                     lp["w_ple_gate"], lp["w_ple_proj"], seq=seq, tm=tiles["tm_ffn"])


TILES = dict(tm=512, tn=1024, tq=512, tk=512, ts=512, tm_ffn=256)


def _trunk(x, p, layers, tiles):
    batch, seq, _ = x.shape
    rope = _rope_tables(seq)
    h = x.reshape(batch * seq, D_MODEL)
    for i, lp in enumerate(layers):
        lam_init = 0.8 - 0.6 * math.exp(-0.3 * i)
        h = _layer(h, p[i].reshape(batch * seq, PLE_DIM), lp, rope, lam_init,
                   batch=batch, seq=seq, tiles=tiles)
    return h.reshape(batch, seq, D_MODEL)


def kernel(x_prompt, x_sample, p_prompt, p_sample, norm_mix_pre, norm_mix_post, w_in, lam_q1, lam_k1, lam_q2, lam_k2, subln_g, lru_conv_w, lru_conv_b, rg_wa, rg_ba, rg_wx, rg_bx, rg_lambda, w_branch_attn, w_branch_lru, w_out, norm_ffn_pre, norm_ffn_post, w_ffn_up, ffn_conv_w, ffn_conv_b, w_ffn_down, w_ple_proj, w_ple_gate):
    w = dict(norm_mix_pre=norm_mix_pre, norm_mix_post=norm_mix_post, w_in=w_in, lam_q1=lam_q1,
             lam_k1=lam_k1, lam_q2=lam_q2, lam_k2=lam_k2, subln_g=subln_g, lru_conv_w=lru_conv_w,
             lru_conv_b=lru_conv_b, rg_wa=rg_wa, rg_ba=rg_ba, rg_wx=rg_wx, rg_bx=rg_bx,
             rg_lambda=rg_lambda, w_branch_attn=w_branch_attn, w_branch_lru=w_branch_lru, w_out=w_out,
             norm_ffn_pre=norm_ffn_pre, norm_ffn_post=norm_ffn_post, w_ffn_up=w_ffn_up,
             ffn_conv_w=ffn_conv_w, ffn_conv_b=ffn_conv_b, w_ffn_down=w_ffn_down,
             w_ple_proj=w_ple_proj, w_ple_gate=w_ple_gate)
    layers = [_prep_layer(i, w) for i in range(w_in.shape[0])]
    return (_trunk(x_prompt, p_prompt, layers, TILES), _trunk(x_sample, p_sample, layers, TILES))
```

```python
import functools
import math

import jax
import jax.numpy as jnp
from jax import lax
from jax.experimental import pallas as pl
from jax.experimental.pallas import tpu as pltpu

F32 = jnp.float32
BF16 = jnp.bfloat16

D_MODEL = 1024
N_HEADS = 8
HEAD_DIM = 64
V_DIM = 2 * HEAD_DIM
ATT_W = N_HEADS * V_DIM
LRU_W = 1024
N_LRU_BLOCKS = 8
LRU_BLOCK = LRU_W // N_LRU_BLOCKS
LRU_CONV_LEFT = 2
LRU_C = 8.0
D_FF = 3072
PLE_DIM = 256
EPS = 1e-6
ROPE_THETA = 10000.0
IN_COLS = 3 * ATT_W + 2 * LRU_W + 2 * D_MODEL

LANES = 128
SUBLANES = 8
BF16_ROWS = 16
VMEM_LIMIT_BYTES = 56 * 1024 * 1024

COL_Q, COL_K, COL_V, COL_U, COL_GL, COL_GA, COL_GB = range(7)


def _cparams(*sem):
    return pltpu.CompilerParams(dimension_semantics=sem, vmem_limit_bytes=VMEM_LIMIT_BYTES)


def _rms(x, g):
    return x * lax.rsqrt(jnp.mean(x * x, axis=-1, keepdims=True) + EPS) * g


def _norm_matmul_kernel(*refs, n_rope_tiles, tn):
    if n_rope_tiles:
        x_ref, g_ref, w_ref, cos_ref, sa_ref, sb_ref, o_ref, h_scr = refs
    else:
        x_ref, g_ref, w_ref, o_ref, h_scr = refs
    j = pl.program_id(1)

    @pl.when(j == 0)
    def _():
        h_scr[...] = _rms(x_ref[...], g_ref[...]).astype(BF16)

    z = jnp.dot(h_scr[...], w_ref[...], preferred_element_type=F32)
    if n_rope_tiles:
        @pl.when(j < n_rope_tiles)
        def _():
            cos, sa, sb = cos_ref[...], sa_ref[...], sb_ref[...]
            for g in range(tn // LANES):
                zg = z[:, g * LANES:(g + 1) * LANES]
                r = (zg * cos + pltpu.roll(zg, LANES - HEAD_DIM // 2, 1) * sa
                     + pltpu.roll(zg, HEAD_DIM // 2, 1) * sb)
                o_ref[:, g * LANES:(g + 1) * LANES] = r.astype(o_ref.dtype)

        @pl.when(j >= n_rope_tiles)
        def _():
            o_ref[...] = z.astype(o_ref.dtype)
    else:
        o_ref[...] = z.astype(o_ref.dtype)


def _norm_matmul(x, g, w, rope, *, seq, tm, tn):
    T, K = x.shape
    N = w.shape[1]
    tm = min(tm, seq)
    n_rope_tiles = (2 * ATT_W) // tn if rope is not None else 0
    in_specs = [
        pl.BlockSpec((tm, K), lambda i, j: (i, 0)),
        pl.BlockSpec((1, K), lambda i, j: (0, 0)),
        pl.BlockSpec((K, tn), lambda i, j: (0, j)),
    ]
    args = [x, g, w]
    if rope is not None:
        nseq = seq // tm
        tab_spec = pl.BlockSpec((tm, LANES), lambda i, j: (i % nseq, 0))
        in_specs += [tab_spec] * 3
        args += list(rope)
    return pl.pallas_call(
        functools.partial(_norm_matmul_kernel, n_rope_tiles=n_rope_tiles, tn=tn),
        grid=(T // tm, N // tn),
        in_specs=in_specs,
        out_specs=pl.BlockSpec((tm, tn), lambda i, j: (i, j)),
        out_shape=jax.ShapeDtypeStruct((T, N), BF16),
        scratch_shapes=[pltpu.VMEM((tm, K), BF16)],
        compiler_params=_cparams("parallel", "arbitrary"),
        name="norm_matmul_rope" if rope is not None else "norm_matmul",
    )(*args)


def _attn_kernel(lamp_ref, g_ref, q_ref, k_ref, v_ref, o_ref, qt_scr, vt_scr, acc_scr, st_scr, *,
                 lam_init, seq, tk):
    tq = q_ref.shape[0]

    @pl.when(pl.program_id(2) == 0)
    def _():
        for c in range(seq // tk):
            sl = slice(c * tk, (c + 1) * tk)
            vt_scr[:, sl] = v_ref[sl, :].astype(F32).T.astype(BF16)

    qt = q_ref[...].astype(F32).T
    first_map = lax.broadcasted_iota(jnp.int32, qt.shape, 0) < HEAD_DIM
    qt_scr[0] = jnp.where(first_map, qt, 0.0).astype(BF16)
    qt_scr[1] = jnp.where(first_map, 0.0, qt).astype(BF16)
    acc_scr[...] = jnp.zeros(acc_scr.shape, F32)
    n_chunks = seq // tk

    def scores(c, slot):
        r0 = pl.multiple_of(jnp.minimum(c, n_chunks - 1) * tk, tk)
        k_c = k_ref[pl.ds(r0, tk), :]
        for mp in range(2):
            st_scr[slot, mp] = jnp.dot(k_c, qt_scr[mp], preferred_element_type=F32)

    def softmax_pv(c, slot, carry):
        ms, ls = carry
        r0 = pl.multiple_of(c * tk, tk)
        vt_c = vt_scr[:, pl.ds(r0, tk)]
        new_m, new_l = [], []
        for mp in range(2):
            m_new = jnp.maximum(ms[mp], jnp.max(st_scr[slot, mp], axis=0, keepdims=True))
            alpha = jnp.exp(ms[mp] - m_new)
            et = jnp.exp(st_scr[slot, mp] - m_new)
            new_l.append(alpha * ls[mp] + jnp.sum(et, axis=0, keepdims=True))
            acc_scr[mp] = alpha * acc_scr[mp] + jnp.dot(vt_c, et.astype(BF16),
                                                        preferred_element_type=F32)
            new_m.append(m_new)
        return tuple(new_m), tuple(new_l)

    def chunk_pair(p, carry):
        c = 2 * p
        scores(c + 1, 1)
        carry = softmax_pv(c, 0, carry)
        scores(c + 2, 0)
        return softmax_pv(c + 1, 1, carry)

    neg_inf = jnp.full((1, tq), -jnp.inf, F32)
    zeros = jnp.zeros((1, tq), F32)
    scores(0, 0)
    _, ls = lax.fori_loop(0, n_chunks // 2, chunk_pair, ((neg_inf, neg_inf), (zeros, zeros)))

    lp = lamp_ref[...]
    lam = (jnp.exp(jnp.sum(lp[0:1] * lp[1:2], axis=-1, keepdims=True))
           - jnp.exp(jnp.sum(lp[2:3] * lp[3:4], axis=-1, keepdims=True)) + lam_init)
    ot = acc_scr[0] / ls[0] - lam * (acc_scr[1] / ls[1])
    o_ref[...] = (_rms(ot.T, g_ref[...]) * (1.0 - lam_init)).astype(o_ref.dtype)


def _attention(z, lamp, subln_g, *, batch, seq, lam_init, tq, tk):
    T = z.shape[0]
    tq, tk = min(tq, seq), min(tk, seq // 2)
    assert seq % (2 * tk) == 0 and seq % tq == 0
    nq = seq // tq
    return pl.pallas_call(
        functools.partial(_attn_kernel, lam_init=lam_init, seq=seq, tk=tk),
        grid=(batch, N_HEADS, nq),
        in_specs=[
            pl.BlockSpec((4, HEAD_DIM), lambda b, h, qi: (0, 0)),
            pl.BlockSpec((1, V_DIM), lambda b, h, qi: (0, 0)),
            pl.BlockSpec((tq, V_DIM), lambda b, h, qi: (b * nq + qi, COL_Q * N_HEADS + h)),
            pl.BlockSpec((seq, V_DIM), lambda b, h, qi: (b, COL_K * N_HEADS + h)),
            pl.BlockSpec((seq, V_DIM), lambda b, h, qi: (b, COL_V * N_HEADS + h)),
        ],
        out_specs=pl.BlockSpec((tq, V_DIM), lambda b, h, qi: (b * nq + qi, h)),
        out_shape=jax.ShapeDtypeStruct((T, ATT_W), BF16),
        scratch_shapes=[pltpu.VMEM((2, V_DIM, tq), BF16), pltpu.VMEM((V_DIM, seq), BF16),
                        pltpu.VMEM((2, V_DIM, tq), F32), pltpu.VMEM((2, 2, tk, tq), F32)],
        compiler_params=_cparams("parallel", "parallel", "arbitrary"),
        name="diff_attention",
    )(lamp, subln_g, z, z, z)


def _softplus(y):
    return jnp.maximum(y, 0.0) + jnp.log1p(jnp.exp(-jnp.abs(y)))


def _lru_kernel(*refs, reverse, ts, ns):
    if reverse:
        (u_ref, up_ref, un_ref, cw_ref, cb_ref, wri_ref, bri_ref, lam_ref, gl_ref, hf_ref,
         o_ref, a_scr, b_scr, h_scr) = refs
    else:
        (u_ref, up_ref, un_ref, cw_ref, cb_ref, wri_ref, bri_ref, lam_ref,
         o_ref, a_scr, b_scr, h_scr) = refs
    j = pl.program_id(1)
    jj = (ns - 1 - j) if reverse else j

    @pl.when(j == 0)
    def _():
        h_scr[...] = jnp.zeros(h_scr.shape, F32)

    u = u_ref[...].astype(F32)
    up = jnp.where(jj > 0, up_ref[...].astype(F32), 0.0)
    un = jnp.where(jj < ns - 1, un_ref[...].astype(F32), 0.0)
    row = lax.broadcasted_iota(jnp.int32, u.shape, 0)
    cw = cw_ref[...]
    xc = cb_ref[...] + u * cw[LRU_CONV_LEFT:LRU_CONV_LEFT + 1]
    for tap in range(cw.shape[0]):
        off = tap - LRU_CONV_LEFT
        if off == 0:
            continue
        sh = pltpu.roll(u, (-off) % ts, 0)
        for e in range(abs(off)):
            if off < 0:
                sh = jnp.where(row == e, up[BF16_ROWS + off + e:BF16_ROWS + off + e + 1], sh)
            else:
                sh = jnp.where(row == ts - off + e, un[e:e + 1], sh)
        xc = xc + sh * cw[tap:tap + 1]
    xcb = xc.astype(BF16)
    neg_c_softplus = -LRU_C * _softplus(-lam_ref[...])
    for n in range(N_LRU_BLOCKS):
        sl = slice(n * LRU_BLOCK, (n + 1) * LRU_BLOCK)
        pre = jnp.dot(xcb[:, sl], wri_ref[n], preferred_element_type=F32) + bri_ref[n]
        r = jax.nn.sigmoid(pre[:, :LRU_BLOCK])
        i = jax.nn.sigmoid(pre[:, LRU_BLOCK:])
        log_a = r * neg_c_softplus[:, sl]
        a = jnp.exp(log_a)
        a_scr[:, sl] = a
        b_scr[:, sl] = jnp.sqrt(-jnp.tanh(log_a) * (a * a + 1.0)) * (i * xc[:, sl])

    rows = lax.broadcasted_iota(jnp.int32, (SUBLANES, LRU_W), 0)
    ng = ts // SUBLANES

    def group(g, h):
        gi = (ng - 1 - g) if reverse else g
        r0 = pl.multiple_of(gi * SUBLANES, SUBLANES)
        a8 = a_scr[pl.ds(r0, SUBLANES), :]
        b8 = b_scr[pl.ds(r0, SUBLANES), :]
        for d in (1, 2, 4):
            shift = (SUBLANES - d) if reverse else d
            keep = (rows < SUBLANES - d) if reverse else (rows >= d)
            a_sh = jnp.where(keep, pltpu.roll(a8, shift, 0), 1.0)
            b_sh = jnp.where(keep, pltpu.roll(b8, shift, 0), 0.0)
            b8 = a8 * b_sh + b8
            a8 = a8 * a_sh
        h8 = a8 * h + b8
        b_scr[pl.ds(r0, SUBLANES), :] = h8
        return h8[0:1, :] if reverse else h8[SUBLANES - 1:SUBLANES, :]

    h_scr[...] = lax.fori_loop(0, ng, group, h_scr[...])
    if reverse:
        hsum = hf_ref[...] + b_scr[...]
        o_ref[...] = (jax.nn.gelu(gl_ref[...].astype(F32)) * hsum).astype(o_ref.dtype)
    else:
        o_ref[...] = b_scr[...]


def _lru(z, conv_w, conv_b, wri, bri, lam, h_fwd, *, batch, seq, ts):
    T = z.shape[0]
    ts = min(ts, seq)
    ns = seq // ts
    reverse = h_fwd is not None
    halo_per_chunk = ts // BF16_ROWS
    last_halo = T // BF16_ROWS - 1

    def chunk(j):
        return (ns - 1 - j) if reverse else j

    def row_block(b, j):
        return b * ns + chunk(j)

    def prev_halo(b, j):
        return (jnp.maximum(row_block(b, j) * halo_per_chunk - 1, 0), COL_U)

    def next_halo(b, j):
        return (jnp.minimum((row_block(b, j) + 1) * halo_per_chunk, last_halo), COL_U)

    const = lambda b, j: (0, 0)
    in_specs = [
        pl.BlockSpec((ts, LRU_W), lambda b, j: (row_block(b, j), COL_U)),
        pl.BlockSpec((BF16_ROWS, LRU_W), prev_halo),
        pl.BlockSpec((BF16_ROWS, LRU_W), next_halo),
        pl.BlockSpec(conv_w.shape, const),
        pl.BlockSpec((1, LRU_W), const),
        pl.BlockSpec(wri.shape, lambda b, j: (0, 0, 0)),
        pl.BlockSpec(bri.shape, lambda b, j: (0, 0, 0)),
        pl.BlockSpec((1, LRU_W), const),
    ]
    args = [z, z, z, conv_w, conv_b, wri, bri, lam]
    if reverse:
        in_specs += [
            pl.BlockSpec((ts, LRU_W), lambda b, j: (row_block(b, j), COL_GL)),
            pl.BlockSpec((ts, LRU_W), lambda b, j: (row_block(b, j), 0)),
        ]
        args += [z, h_fwd]
    return pl.pallas_call(
        functools.partial(_lru_kernel, reverse=reverse, ts=ts, ns=ns),
        grid=(batch, ns),
        in_specs=in_specs,
        out_specs=pl.BlockSpec((ts, LRU_W), lambda b, j: (row_block(b, j), 0)),
        out_shape=jax.ShapeDtypeStruct((T, LRU_W), BF16 if reverse else F32),
        scratch_shapes=[pltpu.VMEM((ts, LRU_W), F32), pltpu.VMEM((ts, LRU_W), F32),
                        pltpu.VMEM((1, LRU_W), F32)],
        compiler_params=_cparams("parallel", "arbitrary"),
        name="lru_bwd" if reverse else "lru_fwd",
    )(*args)


def _merge_kernel(oa_ref, ol_ref, ga_ref, gb_ref, x_ref, wba_ref, wbl_ref, wo_ref, g_ref, o_ref):
    pa = jnp.dot(oa_ref[...], wba_ref[...], preferred_element_type=F32)
    pb = jnp.dot(ol_ref[...], wbl_ref[...], preferred_element_type=F32)
    m = (jax.nn.sigmoid(ga_ref[...].astype(F32)) * pa
         + jax.nn.sigmoid(gb_ref[...].astype(F32)) * pb)
    y = jnp.dot(m.astype(BF16), wo_ref[...], preferred_element_type=F32)
    o_ref[...] = x_ref[...] + _rms(y, g_ref[...])


def _merge(o_attn, o_lru, z, x, w_ba, w_bl, w_out, g, *, seq, tm):
    T = x.shape[0]
    tm = min(tm, seq)
    row = lambda i: (i, 0)
    const = lambda i: (0, 0)
    wspec = pl.BlockSpec((D_MODEL, D_MODEL), const)
    return pl.pallas_call(
        _merge_kernel,
        grid=(T // tm,),
        in_specs=[
            pl.BlockSpec((tm, ATT_W), row),
            pl.BlockSpec((tm, LRU_W), row),
            pl.BlockSpec((tm, D_MODEL), lambda i: (i, COL_GA)),
            pl.BlockSpec((tm, D_MODEL), lambda i: (i, COL_GB)),
            pl.BlockSpec((tm, D_MODEL), row),
            wspec, wspec, wspec,
            pl.BlockSpec((1, D_MODEL), const),
        ],
        out_specs=pl.BlockSpec((tm, D_MODEL), row),
        out_shape=jax.ShapeDtypeStruct((T, D_MODEL), F32),
        compiler_params=_cparams("parallel"),
        name="merge_out",
    )(o_attn, o_lru, z, z, x, w_ba, w_bl, w_out, g)


def _ffn_down_kernel(c_ref, cp_ref, cn_ref, lin_ref, x_ref, p_ref, cw_ref, cb_ref, wd_ref, g_ref,
                     wg_ref, wp_ref, o_ref, *, tm, nseq):
    pos = pl.program_id(0) % nseq
    c = c_ref[...].astype(F32)
    cp = jnp.where(pos > 0, cp_ref[BF16_ROWS - 1:BF16_ROWS, :].astype(F32), 0.0)
    cn = jnp.where(pos < nseq - 1, cn_ref[0:1, :].astype(F32), 0.0)
    row = lax.broadcasted_iota(jnp.int32, c.shape, 0)
    c_prev = jnp.where(row == 0, cp, pltpu.roll(c, 1, 0))
    c_next = jnp.where(row == tm - 1, cn, pltpu.roll(c, tm - 1, 0))
    cw = cw_ref[...]
    conv = c_prev * cw[0:1] + c * cw[1:2] + c_next * cw[2:3] + cb_ref[...]
    act = (jax.nn.gelu(conv) * lin_ref[...].astype(F32)).astype(BF16)
    f = jnp.dot(act, wd_ref[...], preferred_element_type=F32)
    x = x_ref[...] + _rms(f, g_ref[...])
    gate = jax.nn.sigmoid(jnp.dot(x.astype(BF16), wg_ref[...], preferred_element_type=F32))
    emb = jnp.dot(p_ref[...].astype(BF16), wp_ref[...], preferred_element_type=F32)
    o_ref[...] = x + gate * emb


def _ffn_down(cl, x, p, conv_w, conv_b, w_down, g, w_gate, w_proj, *, seq, tm):
    T = x.shape[0]
    tm = min(tm, seq)
    nseq = seq // tm
    halo_per_tile = tm // BF16_ROWS
    last_halo = T // BF16_ROWS - 1
    row = lambda i: (i, 0)
    const = lambda i: (0, 0)
    return pl.pallas_call(
        functools.partial(_ffn_down_kernel, tm=tm, nseq=nseq),
        grid=(T // tm,),
        in_specs=[
            pl.BlockSpec((tm, D_FF), row),
            pl.BlockSpec((BF16_ROWS, D_FF), lambda i: (jnp.maximum(i * halo_per_tile - 1, 0), 0)),
            pl.BlockSpec((BF16_ROWS, D_FF),
                         lambda i: (jnp.minimum((i + 1) * halo_per_tile, last_halo), 0)),
            pl.BlockSpec((tm, D_FF), lambda i: (i, 1)),
            pl.BlockSpec((tm, D_MODEL), row),
            pl.BlockSpec((tm, PLE_DIM), row),
            pl.BlockSpec(conv_w.shape, const),
            pl.BlockSpec((1, D_FF), const),
            pl.BlockSpec((D_FF, D_MODEL), const),
            pl.BlockSpec((1, D_MODEL), const),
            pl.BlockSpec((D_MODEL, D_MODEL), const),
            pl.BlockSpec((PLE_DIM, D_MODEL), const),
        ],
        out_specs=pl.BlockSpec((tm, D_MODEL), row),
        out_shape=jax.ShapeDtypeStruct((T, D_MODEL), F32),
        compiler_params=_cparams("parallel"),
        name="ffn_down_ple",
    )(cl, cl, cl, cl, x, p, conv_w, conv_b, w_down, g, w_gate, w_proj)


def _rope_tables(seq):
    half = HEAD_DIM // 2
    inv = ROPE_THETA ** (-jnp.arange(0, HEAD_DIM, 2, dtype=F32) / HEAD_DIM)
    ang = jnp.arange(seq, dtype=F32)[:, None] * inv[None, :]
    lane = jnp.arange(LANES)
    cos = jnp.cos(ang)[:, lane % half]
    sin = jnp.sin(ang)[:, lane % half]
    lower = (lane % HEAD_DIM) < half
    return cos, jnp.where(lower, -sin, 0.0), jnp.where(lower, 0.0, sin)


def _prep_layer(i, w):
    q_scale = jnp.where(jnp.arange(IN_COLS) < ATT_W, HEAD_DIM ** -0.5, 1.0).astype(F32)
    row = lambda a: a.reshape(1, -1).astype(F32)
    wri = jnp.concatenate([w["rg_wa"][i], w["rg_wx"][i]], axis=-1).astype(BF16)
    bri = jnp.concatenate([w["rg_ba"][i].reshape(2, N_LRU_BLOCKS, 1, LRU_BLOCK),
                           w["rg_bx"][i].reshape(2, N_LRU_BLOCKS, 1, LRU_BLOCK)], axis=-1)
    return dict(
        norm_mix_pre=row(w["norm_mix_pre"][i]),
        norm_mix_post=row(w["norm_mix_post"][i]),
        w_in=(w["w_in"][i] * q_scale).astype(BF16),
        lamp=jnp.stack([w["lam_q1"][i], w["lam_k1"][i], w["lam_q2"][i], w["lam_k2"][i]]).astype(F32),
        subln_g=row(w["subln_g"][i]),
        lru_conv_w=w["lru_conv_w"][i].astype(F32),
        lru_conv_b=row(w["lru_conv_b"][i]),
        wri=wri, bri=bri.astype(F32),
        rg_lambda=w["rg_lambda"][i].reshape(2, 1, LRU_W).astype(F32),
        w_branch_attn=w["w_branch_attn"][i].astype(BF16),
        w_branch_lru=w["w_branch_lru"][i].astype(BF16),
        w_out=w["w_out"][i].astype(BF16),
        norm_ffn_pre=row(w["norm_ffn_pre"][i]),
        norm_ffn_post=row(w["norm_ffn_post"][i]),
        w_ffn_up=w["w_ffn_up"][i].astype(BF16),
        ffn_conv_w=w["ffn_conv_w"][i].astype(F32),
        ffn_conv_b=row(w["ffn_conv_b"][i]),
        w_ffn_down=w["w_ffn_down"][i].astype(BF16),
        w_ple_proj=w["w_ple_proj"][i].astype(BF16),
        w_ple_gate=w["w_ple_gate"][i].astype(BF16),
    )


def _layer(x, p, lp, rope, lam_init, *, batch, seq, tiles):
    z = _norm_matmul(x, lp["norm_mix_pre"], lp["w_in"], rope, seq=seq, tm=tiles["tm"], tn=tiles["tn"])
    o_attn = _attention(z, lp["lamp"], lp["subln_g"], batch=batch, seq=seq, lam_init=lam_init,
                        tq=tiles["tq"], tk=tiles["tk"])
    h_fwd = _lru(z, lp["lru_conv_w"], lp["lru_conv_b"], lp["wri"][0], lp["bri"][0], lp["rg_lambda"][0],
                 None, batch=batch, seq=seq, ts=tiles["ts"])
    o_lru = _lru(z, lp["lru_conv_w"], lp["lru_conv_b"], lp["wri"][1], lp["bri"][1], lp["rg_lambda"][1],
                 h_fwd, batch=batch, seq=seq, ts=tiles["ts"])
    x1 = _merge(o_attn, o_lru, z, x, lp["w_branch_attn"], lp["w_branch_lru"], lp["w_out"],
                lp["norm_mix_post"], seq=seq, tm=tiles["tm"])
    cl = _norm_matmul(x1, lp["norm_ffn_pre"], lp["w_ffn_up"], None, seq=seq, tm=tiles["tm"], tn=tiles["tn"])
    return _ffn_down(cl, x1, p, lp["ffn_conv_w"], lp["ffn_conv_b"], lp["w_ffn_down"], lp["norm_ffn_post"],
                     lp["w_ple_gate"], lp["w_ple_proj"], seq=seq, tm=tiles["tm_ffn"])


TILES = dict(tm=512, tn=1024, tq=512, tk=512, ts=512, tm_ffn=256)


def _trunk(x, p, layers, tiles):
    batch, seq, _ = x.shape
    rope = _rope_tables(seq)
    h = x.reshape(batch * seq, D_MODEL)
    for i, lp in enumerate(layers):
        lam_init = 0.8 - 0.6 * math.exp(-0.3 * i)
        h = _layer(h, p[i].reshape(batch * seq, PLE_DIM), lp, rope, lam_init,
                   batch=batch, seq=seq, tiles=tiles)
    return h.reshape(batch, seq, D_MODEL)


def kernel(x_prompt, x_sample, p_prompt, p_sample, norm_mix_pre, norm_mix_post, w_in, lam_q1, lam_k1, lam_q2, lam_k2, subln_g, lru_conv_w, lru_conv_b, rg_wa, rg_ba, rg_wx, rg_bx, rg_lambda, w_branch_attn, w_branch_lru, w_out, norm_ffn_pre, norm_ffn_post, w_ffn_up, ffn_conv_w, ffn_conv_b, w_ffn_down, w_ple_proj, w_ple_gate):
    w = dict(norm_mix_pre=norm_mix_pre, norm_mix_post=norm_mix_post, w_in=w_in, lam_q1=lam_q1,
             lam_k1=lam_k1, lam_q2=lam_q2, lam_k2=lam_k2, subln_g=subln_g, lru_conv_w=lru_conv_w,
             lru_conv_b=lru_conv_b, rg_wa=rg_wa, rg_ba=rg_ba, rg_wx=rg_wx, rg_bx=rg_bx,
             rg_lambda=rg_lambda, w_branch_attn=w_branch_attn, w_branch_lru=w_branch_lru, w_out=w_out,
             norm_ffn_pre=norm_ffn_pre, norm_ffn_post=norm_ffn_post, w_ffn_up=w_ffn_up,
             ffn_conv_w=ffn_conv_w, ffn_conv_b=ffn_conv_b, w_ffn_down=w_ffn_down,
             w_ple_proj=w_ple_proj, w_ple_gate=w_ple_gate)
    layers = [_prep_layer(i, w) for i in range(w_in.shape[0])]
    return (_trunk(x_prompt, p_prompt, layers, TILES), _trunk(x_sample, p_sample, layers, TILES))
```

```python
import functools
import math

import jax
import jax.numpy as jnp
from jax import lax
from jax.experimental import pallas as pl
from jax.experimental.pallas import tpu as pltpu

F32 = jnp.float32
BF16 = jnp.bfloat16

D_MODEL = 1024
N_HEADS = 8
HEAD_DIM = 64
V_DIM = 2 * HEAD_DIM
ATT_W = N_HEADS * V_DIM
LRU_W = 1024
N_LRU_BLOCKS = 8
LRU_BLOCK = LRU_W // N_LRU_BLOCKS
LRU_CONV_LEFT = 2
LRU_C = 8.0
D_FF = 3072
PLE_DIM = 256
EPS = 1e-6
ROPE_THETA = 10000.0
IN_COLS = 3 * ATT_W + 2 * LRU_W + 2 * D_MODEL
LOG2_E = math.log2(math.e)

LANES = 128
SUBLANES = 8
BF16_ROWS = 16
VMEM_LIMIT_BYTES = 56 * 1024 * 1024

COL_Q, COL_K, COL_V, COL_U, COL_GL, COL_GA, COL_GB = range(7)


def _cparams(*sem):
    return pltpu.CompilerParams(dimension_semantics=sem, vmem_limit_bytes=VMEM_LIMIT_BYTES)


def _rms(x, g):
    return x * lax.rsqrt(jnp.mean(x * x, axis=-1, keepdims=True) + EPS) * g


def _norm_matmul_kernel(*refs, rope, tn):
    if rope:
        x_ref, g_ref, w_ref, cos_ref, sa_ref, sb_ref, o_ref, h_scr = refs
    else:
        x_ref, g_ref, w_ref, o_ref, h_scr = refs
    h_scr[...] = _rms(x_ref[...], g_ref[...]).astype(BF16)
    for j in range(w_ref.shape[1] // tn):
        col0 = j * tn
        z = jnp.dot(h_scr[...], w_ref[:, col0:col0 + tn], preferred_element_type=F32)
        if rope and col0 < 2 * ATT_W:
            scale = LOG2_E if col0 < ATT_W else 1.0
            cos, sa, sb = cos_ref[...] * scale, sa_ref[...] * scale, sb_ref[...] * scale
            for g in range(tn // LANES):
                zg = z[:, g * LANES:(g + 1) * LANES]
                r = (zg * cos + pltpu.roll(zg, LANES - HEAD_DIM // 2, 1) * sa
                     + pltpu.roll(zg, HEAD_DIM // 2, 1) * sb)
                o_ref[:, col0 + g * LANES:col0 + (g + 1) * LANES] = r.astype(o_ref.dtype)
        else:
            o_ref[:, col0:col0 + tn] = z.astype(o_ref.dtype)


def _norm_matmul(x, g, w, rope, *, seq, tm, tn):
    T, K = x.shape
    N = w.shape[1]
    tm = min(tm, seq)
    assert N % tn == 0 and ATT_W % tn == 0
    const = lambda i: (0, 0)
    in_specs = [
        pl.BlockSpec((tm, K), lambda i: (i, 0)),
        pl.BlockSpec((1, K), const),
        pl.BlockSpec((K, N), const, pipeline_mode=pl.Buffered(1)),
    ]
    args = [x, g, w]
    if rope is not None:
        nseq = seq // tm
        tab_spec = pl.BlockSpec((tm, LANES), lambda i: (i % nseq, 0))
        in_specs += [tab_spec] * 3
        args += list(rope)
    return pl.pallas_call(
        functools.partial(_norm_matmul_kernel, rope=rope is not None, tn=tn),
        grid=(T // tm,),
        in_specs=in_specs,
        out_specs=pl.BlockSpec((tm, N), lambda i: (i, 0)),
        out_shape=jax.ShapeDtypeStruct((T, N), BF16),
        scratch_shapes=[pltpu.VMEM((tm, K), BF16)],
        compiler_params=_cparams("parallel"),
        name="norm_matmul_rope" if rope is not None else "norm_matmul",
    )(*args)


def _attn_kernel(lamp_ref, g_ref, q_ref, k_ref, v_ref, o_ref, qt_scr, vt_scr, acc_scr, st_scr, mx_scr,
                 *, lam_init, seq, tk):
    tq = q_ref.shape[0]

    @pl.when(pl.program_id(2) == 0)
    def _():
        vt_scr[V_DIM:, :] = jnp.ones((BF16_ROWS, seq), BF16)
        for c in range(seq // tk):
            sl = slice(c * tk, (c + 1) * tk)
            vt_scr[:V_DIM, sl] = v_ref[sl, :].astype(F32).T.astype(BF16)

    qt = q_ref[...].astype(F32).T
    first_map = lax.broadcasted_iota(jnp.int32, qt.shape, 0) < HEAD_DIM
    qt_scr[0] = jnp.where(first_map, qt, 0.0).astype(BF16)
    qt_scr[1] = jnp.where(first_map, 0.0, qt).astype(BF16)
    acc_scr[...] = jnp.zeros(acc_scr.shape, F32)
    n_chunks = seq // tk

    def scores(c, slot):
        r0 = pl.multiple_of(jnp.minimum(c, n_chunks - 1) * tk, tk)
        k_c = k_ref[pl.ds(r0, tk), :]
        for mp in range(2):
            st = jnp.dot(k_c, qt_scr[mp], preferred_element_type=F32)
            st_scr[slot, mp] = st
            mx_scr[slot, mp] = jnp.max(st, axis=0, keepdims=True)

    def softmax_pv(c, slot, ms):
        r0 = pl.multiple_of(c * tk, tk)
        vt_c = vt_scr[:, pl.ds(r0, tk)]
        new_m = []
        for mp in range(2):
            m_new = jnp.maximum(ms[mp], mx_scr[slot, mp])
            alpha = jnp.exp2(ms[mp] - m_new)
            et = jnp.exp2(st_scr[slot, mp] - m_new).astype(BF16)
            acc_scr[mp] = alpha * acc_scr[mp] + jnp.dot(vt_c, et, preferred_element_type=F32)
            new_m.append(m_new)
        return tuple(new_m)

    def chunk_pair(p, ms):
        c = 2 * p
        scores(c + 1, 1)
        ms = softmax_pv(c, 0, ms)
        scores(c + 2, 0)
        return softmax_pv(c + 1, 1, ms)

    neg_inf = jnp.full((1, tq), -jnp.inf, F32)
    scores(0, 0)
    lax.fori_loop(0, n_chunks // 2, chunk_pair, (neg_inf, neg_inf))

    lp = lamp_ref[...]
    lam = (jnp.exp(jnp.sum(lp[0:1] * lp[1:2], axis=-1, keepdims=True))
           - jnp.exp(jnp.sum(lp[2:3] * lp[3:4], axis=-1, keepdims=True)) + lam_init)
    acc0, acc1 = acc_scr[0], acc_scr[1]
    ot = (acc0[:V_DIM] / acc0[V_DIM:V_DIM + 1]
          - lam * (acc1[:V_DIM] / acc1[V_DIM:V_DIM + 1]))
    o_ref[...] = (_rms(ot.T, g_ref[...]) * (1.0 - lam_init)).astype(o_ref.dtype)


def _attention(z, lamp, subln_g, *, batch, seq, lam_init, tq, tk):
    T = z.shape[0]
    tq, tk = min(tq, seq), min(tk, seq // 2)
    assert seq % (2 * tk) == 0 and seq % tq == 0
    nq = seq // tq
    return pl.pallas_call(
        functools.partial(_attn_kernel, lam_init=lam_init, seq=seq, tk=tk),
        grid=(batch, N_HEADS, nq),
        in_specs=[
            pl.BlockSpec((4, HEAD_DIM), lambda b, h, qi: (0, 0)),
            pl.BlockSpec((1, V_DIM), lambda b, h, qi: (0, 0)),
            pl.BlockSpec((tq, V_DIM), lambda b, h, qi: (b * nq + qi, COL_Q * N_HEADS + h)),
            pl.BlockSpec((seq, V_DIM), lambda b, h, qi: (b, COL_K * N_HEADS + h)),
            pl.BlockSpec((seq, V_DIM), lambda b, h, qi: (b, COL_V * N_HEADS + h)),
        ],
        out_specs=pl.BlockSpec((tq, V_DIM), lambda b, h, qi: (b * nq + qi, h)),
        out_shape=jax.ShapeDtypeStruct((T, ATT_W), BF16),
        scratch_shapes=[pltpu.VMEM((2, V_DIM, tq), BF16),
                        pltpu.VMEM((V_DIM + BF16_ROWS, seq), BF16),
                        pltpu.VMEM((2, V_DIM + BF16_ROWS, tq), F32),
                        pltpu.VMEM((2, 2, tk, tq), F32),
                        pltpu.VMEM((2, 2, 1, tq), F32)],
        compiler_params=_cparams("parallel", "parallel", "arbitrary"),
        name="diff_attention",
    )(lamp, subln_g, z, z, z)


def _softplus(y):
    return jnp.maximum(y, 0.0) + jnp.log1p(jnp.exp(-jnp.abs(y)))


def _lru_kernel(*refs, reverse, ts, ns):
    if reverse:
        (u_ref, up_ref, un_ref, cw_ref, cb_ref, wri_ref, bri_ref, lam_ref, gl_ref, hf_ref,
         o_ref, a_scr, b_scr, h_scr) = refs
    else:
        (u_ref, up_ref, un_ref, cw_ref, cb_ref, wri_ref, bri_ref, lam_ref,
         o_ref, a_scr, b_scr, h_scr) = refs
    j = pl.program_id(1)
    jj = (ns - 1 - j) if reverse else j

    @pl.when(j == 0)
    def _():
        h_scr[...] = jnp.zeros(h_scr.shape, F32)

    u = u_ref[...].astype(F32)
    up = jnp.where(jj > 0, up_ref[...].astype(F32), 0.0)
    un = jnp.where(jj < ns - 1, un_ref[...].astype(F32), 0.0)
    row = lax.broadcasted_iota(jnp.int32, u.shape, 0)
    cw = cw_ref[...]
    xc = cb_ref[...] + u * cw[LRU_CONV_LEFT:LRU_CONV_LEFT + 1]
    for tap in range(cw.shape[0]):
        off = tap - LRU_CONV_LEFT
        if off == 0:
            continue
        sh = pltpu.roll(u, (-off) % ts, 0)
        for e in range(abs(off)):
            if off < 0:
                sh = jnp.where(row == e, up[BF16_ROWS + off + e:BF16_ROWS + off + e + 1], sh)
            else:
                sh = jnp.where(row == ts - off + e, un[e:e + 1], sh)
        xc = xc + sh * cw[tap:tap + 1]
    xcb = xc.astype(BF16)
    neg_c_softplus = -LRU_C * _softplus(-lam_ref[...])
    for n in range(N_LRU_BLOCKS):
        sl = slice(n * LRU_BLOCK, (n + 1) * LRU_BLOCK)
        pre = jnp.dot(xcb[:, sl], wri_ref[n], preferred_element_type=F32) + bri_ref[n]
        r = jax.nn.sigmoid(pre[:, :LRU_BLOCK])
        i = jax.nn.sigmoid(pre[:, LRU_BLOCK:])
        log_a = r * neg_c_softplus[:, sl]
        a = jnp.exp(log_a)
        a_scr[:, sl] = a
        b_scr[:, sl] = jnp.sqrt(-jnp.tanh(log_a) * (a * a + 1.0)) * (i * xc[:, sl])

    rows = lax.broadcasted_iota(jnp.int32, (SUBLANES, LRU_W), 0)
    ng = ts // SUBLANES

    def group(g, h):
        gi = (ng - 1 - g) if reverse else g
        r0 = pl.multiple_of(gi * SUBLANES, SUBLANES)
        a8 = a_scr[pl.ds(r0, SUBLANES), :]
        b8 = b_scr[pl.ds(r0, SUBLANES), :]
        for d in (1, 2, 4):
            shift = (SUBLANES - d) if reverse else d
            keep = (rows < SUBLANES - d) if reverse else (rows >= d)
            a_sh = jnp.where(keep, pltpu.roll(a8, shift, 0), 1.0)
            b_sh = jnp.where(keep, pltpu.roll(b8, shift, 0), 0.0)
            b8 = a8 * b_sh + b8
            a8 = a8 * a_sh
        h8 = a8 * h + b8
        b_scr[pl.ds(r0, SUBLANES), :] = h8
        return h8[0:1, :] if reverse else h8[SUBLANES - 1:SUBLANES, :]

    h_scr[...] = lax.fori_loop(0, ng, group, h_scr[...])
    if reverse:
        hsum = hf_ref[...] + b_scr[...]
        o_ref[...] = (jax.nn.gelu(gl_ref[...].astype(F32)) * hsum).astype(o_ref.dtype)
    else:
        o_ref[...] = b_scr[...]


def _lru(z, conv_w, conv_b, wri, bri, lam, h_fwd, *, batch, seq, ts):
    T = z.shape[0]
    ts = min(ts, seq)
    ns = seq // ts
    reverse = h_fwd is not None
    halo_per_chunk = ts // BF16_ROWS
    last_halo = T // BF16_ROWS - 1

    def chunk(j):
        return (ns - 1 - j) if reverse else j

    def row_block(b, j):
        return b * ns + chunk(j)

    def prev_halo(b, j):
        return (jnp.maximum(row_block(b, j) * halo_per_chunk - 1, 0), COL_U)

    def next_halo(b, j):
        return (jnp.minimum((row_block(b, j) + 1) * halo_per_chunk, last_halo), COL_U)

    const = lambda b, j: (0, 0)
    in_specs = [
        pl.BlockSpec((ts, LRU_W), lambda b, j: (row_block(b, j), COL_U)),
        pl.BlockSpec((BF16_ROWS, LRU_W), prev_halo),
        pl.BlockSpec((BF16_ROWS, LRU_W), next_halo),
        pl.BlockSpec(conv_w.shape, const),
        pl.BlockSpec((1, LRU_W), const),
        pl.BlockSpec(wri.shape, lambda b, j: (0, 0, 0)),
        pl.BlockSpec(bri.shape, lambda b, j: (0, 0, 0)),
        pl.BlockSpec((1, LRU_W), const),
    ]
    args = [z, z, z, conv_w, conv_b, wri, bri, lam]
    if reverse:
        in_specs += [
            pl.BlockSpec((ts, LRU_W), lambda b, j: (row_block(b, j), COL_GL)),
            pl.BlockSpec((ts, LRU_W), lambda b, j: (row_block(b, j), 0)),
        ]
        args += [z, h_fwd]
    return pl.pallas_call(
        functools.partial(_lru_kernel, reverse=reverse, ts=ts, ns=ns),
        grid=(batch, ns),
        in_specs=in_specs,
        out_specs=pl.BlockSpec((ts, LRU_W), lambda b, j: (row_block(b, j), 0)),
        out_shape=jax.ShapeDtypeStruct((T, LRU_W), BF16 if reverse else F32),
        scratch_shapes=[pltpu.VMEM((ts, LRU_W), F32), pltpu.VMEM((ts, LRU_W), F32),
                        pltpu.VMEM((1, LRU_W), F32)],
        compiler_params=_cparams("parallel", "arbitrary"),
        name="lru_bwd" if reverse else "lru_fwd",
    )(*args)


def _merge_kernel(oa_ref, ol_ref, ga_ref, gb_ref, x_ref, wba_ref, wbl_ref, wo_ref, g_ref, o_ref):
    pa = jnp.dot(oa_ref[...], wba_ref[...], preferred_element_type=F32)
    pb = jnp.dot(ol_ref[...], wbl_ref[...], preferred_element_type=F32)
    m = (jax.nn.sigmoid(ga_ref[...].astype(F32)) * pa
         + jax.nn.sigmoid(gb_ref[...].astype(F32)) * pb)
    y = jnp.dot(m.astype(BF16), wo_ref[...], preferred_element_type=F32)
    o_ref[...] = x_ref[...] + _rms(y, g_ref[...])


def _merge(o_attn, o_lru, z, x, w_ba, w_bl, w_out, g, *, seq, tm):
    T = x.shape[0]
    tm = min(tm, seq)
    row = lambda i: (i, 0)
    const = lambda i: (0, 0)
    wspec = pl.BlockSpec((D_MODEL, D_MODEL), const)
    return pl.pallas_call(
        _merge_kernel,
        grid=(T // tm,),
        in_specs=[
            pl.BlockSpec((tm, ATT_W), row),
            pl.BlockSpec((tm, LRU_W), row),
            pl.BlockSpec((tm, D_MODEL), lambda i: (i, COL_GA)),
            pl.BlockSpec((tm, D_MODEL), lambda i: (i, COL_GB)),
            pl.BlockSpec((tm, D_MODEL), row),
            wspec, wspec, wspec,
            pl.BlockSpec((1, D_MODEL), const),
        ],
        out_specs=pl.BlockSpec((tm, D_MODEL), row),
        out_shape=jax.ShapeDtypeStruct((T, D_MODEL), F32),
        compiler_params=_cparams("parallel"),
        name="merge_out",
    )(o_attn, o_lru, z, z, x, w_ba, w_bl, w_out, g)


def _ffn_down_kernel(c_ref, cp_ref, cn_ref, lin_ref, x_ref, p_ref, cw_ref, cb_ref, wd_ref, g_ref,
                     wg_ref, wp_ref, o_ref, *, tm, nseq):
    pos = pl.program_id(0) % nseq
    c = c_ref[...].astype(F32)
    cp = jnp.where(pos > 0, cp_ref[BF16_ROWS - 1:BF16_ROWS, :].astype(F32), 0.0)
    cn = jnp.where(pos < nseq - 1, cn_ref[0:1, :].astype(F32), 0.0)
    row = lax.broadcasted_iota(jnp.int32, c.shape, 0)
    c_prev = jnp.where(row == 0, cp, pltpu.roll(c, 1, 0))
    c_next = jnp.where(row == tm - 1, cn, pltpu.roll(c, tm - 1, 0))
    cw = cw_ref[...]
    conv = c_prev * cw[0:1] + c * cw[1:2] + c_next * cw[2:3] + cb_ref[...]
    act = (jax.nn.gelu(conv) * lin_ref[...].astype(F32)).astype(BF16)
    f = jnp.dot(act, wd_ref[...], preferred_element_type=F32)
    x = x_ref[...] + _rms(f, g_ref[...])
    gate = jax.nn.sigmoid(jnp.dot(x.astype(BF16), wg_ref[...], preferred_element_type=F32))
    emb = jnp.dot(p_ref[...].astype(BF16), wp_ref[...], preferred_element_type=F32)
    o_ref[...] = x + gate * emb


def _ffn_down(cl, x, p, conv_w, conv_b, w_down, g, w_gate, w_proj, *, seq, tm):
    T = x.shape[0]
    tm = min(tm, seq)
    nseq = seq // tm
    halo_per_tile = tm // BF16_ROWS
    last_halo = T // BF16_ROWS - 1
    row = lambda i: (i, 0)
    const = lambda i: (0, 0)
    return pl.pallas_call(
        functools.partial(_ffn_down_kernel, tm=tm, nseq=nseq),
        grid=(T // tm,),
        in_specs=[
            pl.BlockSpec((tm, D_FF), row),
            pl.BlockSpec((BF16_ROWS, D_FF), lambda i: (jnp.maximum(i * halo_per_tile - 1, 0), 0)),
            pl.BlockSpec((BF16_ROWS, D_FF),
                         lambda i: (jnp.minimum((i + 1) * halo_per_tile, last_halo), 0)),
            pl.BlockSpec((tm, D_FF), lambda i: (i, 1)),
            pl.BlockSpec((tm, D_MODEL), row),
            pl.BlockSpec((tm, PLE_DIM), row),
            pl.BlockSpec(conv_w.shape, const),
            pl.BlockSpec((1, D_FF), const),
            pl.BlockSpec((D_FF, D_MODEL), const),
            pl.BlockSpec((1, D_MODEL), const),
            pl.BlockSpec((D_MODEL, D_MODEL), const),
            pl.BlockSpec((PLE_DIM, D_MODEL), const),
        ],
        out_specs=pl.BlockSpec((tm, D_MODEL), row),
        out_shape=jax.ShapeDtypeStruct((T, D_MODEL), F32),
        compiler_params=_cparams("parallel"),
        name="ffn_down_ple",
    )(cl, cl, cl, cl, x, p, conv_w, conv_b, w_down, g, w_gate, w_proj)


def _rope_tables(seq):
    half = HEAD_DIM // 2
    inv = ROPE_THETA ** (-jnp.arange(0, HEAD_DIM, 2, dtype=F32) / HEAD_DIM)
    ang = jnp.arange(seq, dtype=F32)[:, None] * inv[None, :]
    lane = jnp.arange(LANES)
    cos = jnp.cos(ang)[:, lane % half]
    sin = jnp.sin(ang)[:, lane % half]
    lower = (lane % HEAD_DIM) < half
    return cos, jnp.where(lower, -sin, 0.0), jnp.where(lower, 0.0, sin)


def _prep_layer(i, w):
    q_scale = jnp.where(jnp.arange(IN_COLS) < ATT_W, HEAD_DIM ** -0.5, 1.0).astype(F32)
    row = lambda a: a.reshape(1, -1).astype(F32)
    wri = jnp.concatenate([w["rg_wa"][i], w["rg_wx"][i]], axis=-1).astype(BF16)
    bri = jnp.concatenate([w["rg_ba"][i].reshape(2, N_LRU_BLOCKS, 1, LRU_BLOCK),
                           w["rg_bx"][i].reshape(2, N_LRU_BLOCKS, 1, LRU_BLOCK)], axis=-1)
    return dict(
        norm_mix_pre=row(w["norm_mix_pre"][i]),
        norm_mix_post=row(w["norm_mix_post"][i]),
        w_in=(w["w_in"][i] * q_scale).astype(BF16),
        lamp=jnp.stack([w["lam_q1"][i], w["lam_k1"][i], w["lam_q2"][i], w["lam_k2"][i]]).astype(F32),
        subln_g=row(w["subln_g"][i]),
        lru_conv_w=w["lru_conv_w"][i].astype(F32),
        lru_conv_b=row(w["lru_conv_b"][i]),
        wri=wri, bri=bri.astype(F32),
        rg_lambda=w["rg_lambda"][i].reshape(2, 1, LRU_W).astype(F32),
        w_branch_attn=w["w_branch_attn"][i].astype(BF16),
        w_branch_lru=w["w_branch_lru"][i].astype(BF16),
        w_out=w["w_out"][i].astype(BF16),
        norm_ffn_pre=row(w["norm_ffn_pre"][i]),
        norm_ffn_post=row(w["norm_ffn_post"][i]),
        w_ffn_up=w["w_ffn_up"][i].astype(BF16),
        ffn_conv_w=w["ffn_conv_w"][i].astype(F32),
        ffn_conv_b=row(w["ffn_conv_b"][i]),
        w_ffn_down=w["w_ffn_down"][i].astype(BF16),
        w_ple_proj=w["w_ple_proj"][i].astype(BF16),
        w_ple_gate=w["w_ple_gate"][i].astype(BF16),
    )


def _layer(x, p, lp, rope, lam_init, *, batch, seq, tiles):
    z = _norm_matmul(x, lp["norm_mix_pre"], lp["w_in"], rope, seq=seq, tm=tiles["tm"], tn=tiles["tn"])
    o_attn = _attention(z, lp["lamp"], lp["subln_g"], batch=batch, seq=seq, lam_init=lam_init,
                        tq=tiles["tq"], tk=tiles["tk"])
    h_fwd = _lru(z, lp["lru_conv_w"], lp["lru_conv_b"], lp["wri"][0], lp["bri"][0], lp["rg_lambda"][0],
                 None, batch=batch, seq=seq, ts=tiles["ts"])
    o_lru = _lru(z, lp["lru_conv_w"], lp["lru_conv_b"], lp["wri"][1], lp["bri"][1], lp["rg_lambda"][1],
                 h_fwd, batch=batch, seq=seq, ts=tiles["ts"])
    x1 = _merge(o_attn, o_lru, z, x, lp["w_branch_attn"], lp["w_branch_lru"], lp["w_out"],
                lp["norm_mix_post"], seq=seq, tm=tiles["tm"])
    cl = _norm_matmul(x1, lp["norm_ffn_pre"], lp["w_ffn_up"], None, seq=seq, tm=tiles["tm"], tn=tiles["tn"])
    return _ffn_down(cl, x1, p, lp["ffn_conv_w"], lp["ffn_conv_b"], lp["w_ffn_down"], lp["norm_ffn_post"],
                     lp["w_ple_gate"], lp["w_ple_proj"], seq=seq, tm=tiles["tm_ffn"])


TILES = dict(tm=512, tn=1024, tq=512, tk=512, ts=512, tm_ffn=256)


def _trunk(x, p, layers, tiles):
    batch, seq, _ = x.shape
    rope = _rope_tables(seq)
    h = x.reshape(batch * seq, D_MODEL)
    for i, lp in enumerate(layers):
        lam_init = 0.8 - 0.6 * math.exp(-0.3 * i)
        h = _layer(h, p[i].reshape(batch * seq, PLE_DIM), lp, rope, lam_init,
                   batch=batch, seq=seq, tiles=tiles)
    return h.reshape(batch, seq, D_MODEL)


def kernel(x_prompt, x_sample, p_prompt, p_sample, norm_mix_pre, norm_mix_post, w_in, lam_q1, lam_k1, lam_q2, lam_k2, subln_g, lru_conv_w, lru_conv_b, rg_wa, rg_ba, rg_wx, rg_bx, rg_lambda, w_branch_attn, w_branch_lru, w_out, norm_ffn_pre, norm_ffn_post, w_ffn_up, ffn_conv_w, ffn_conv_b, w_ffn_down, w_ple_proj, w_ple_gate):
    w = dict(norm_mix_pre=norm_mix_pre, norm_mix_post=norm_mix_post, w_in=w_in, lam_q1=lam_q1,
             lam_k1=lam_k1, lam_q2=lam_q2, lam_k2=lam_k2, subln_g=subln_g, lru_conv_w=lru_conv_w,
             lru_conv_b=lru_conv_b, rg_wa=rg_wa, rg_ba=rg_ba, rg_wx=rg_wx, rg_bx=rg_bx,
             rg_lambda=rg_lambda, w_branch_attn=w_branch_attn, w_branch_lru=w_branch_lru, w_out=w_out,
             norm_ffn_pre=norm_ffn_pre, norm_ffn_post=norm_ffn_post, w_ffn_up=w_ffn_up,
             ffn_conv_w=ffn_conv_w, ffn_conv_b=ffn_conv_b, w_ffn_down=w_ffn_down,
             w_ple_proj=w_ple_proj, w_ple_gate=w_ple_gate)
    layers = [_prep_layer(i, w) for i in range(w_in.shape[0])]
    return (_trunk(x_prompt, p_prompt, layers, TILES), _trunk(x_sample, p_sample, layers, TILES))
```

```python
import functools
import math

import jax
import jax.numpy as jnp
from jax import lax
from jax.experimental import pallas as pl
from jax.experimental.pallas import tpu as pltpu

F32 = jnp.float32
BF16 = jnp.bfloat16

D_MODEL = 1024
N_HEADS = 8
HEAD_DIM = 64
V_DIM = 2 * HEAD_DIM
ATT_W = N_HEADS * V_DIM
LRU_W = 1024
N_LRU_BLOCKS = 8
LRU_BLOCK = LRU_W // N_LRU_BLOCKS
LRU_CONV_LEFT = 2
LRU_C = 8.0
D_FF = 3072
PLE_DIM = 256
EPS = 1e-6
TINY = 1e-30
ROPE_THETA = 10000.0
IN_COLS = 3 * ATT_W + 2 * LRU_W + 2 * D_MODEL
LOG2_E = math.log2(math.e)
KEY_CHUNKS_PER_ITER = 4
FFN_SUB_ROWS = 256
GELU_K0 = math.sqrt(2.0 / math.pi)
GELU_K1 = 0.044715 * GELU_K0

LANES = 128
SUBLANES = 8
BF16_ROWS = 16
VMEM_LIMIT_BYTES = 56 * 1024 * 1024

COL_Q, COL_K, COL_V, COL_U, COL_GL, COL_GA, COL_GB = range(7)


def _cparams(*sem):
    return pltpu.CompilerParams(dimension_semantics=sem, vmem_limit_bytes=VMEM_LIMIT_BYTES)


def _rms(x, g):
    return x * lax.rsqrt(jnp.mean(x * x, axis=-1, keepdims=True) + EPS) * g


def _norm_matmul_kernel(*refs, rope, tn):
    if rope:
        x_ref, g_ref, w_ref, cos_ref, sa_ref, sb_ref, o_ref, h_scr = refs
    else:
        x_ref, g_ref, w_ref, o_ref, h_scr = refs
    h_scr[...] = _rms(x_ref[...], g_ref[...]).astype(BF16)
    for j in range(w_ref.shape[1] // tn):
        col0 = j * tn
        z = jnp.dot(h_scr[...], w_ref[:, col0:col0 + tn], preferred_element_type=F32)
        if rope and col0 < 2 * ATT_W:
            scale = LOG2_E if col0 < ATT_W else 1.0
            cos, sa, sb = cos_ref[...] * scale, sa_ref[...] * scale, sb_ref[...] * scale
            for g in range(tn // LANES):
                zg = z[:, g * LANES:(g + 1) * LANES]
                r = (zg * cos + pltpu.roll(zg, LANES - HEAD_DIM // 2, 1) * sa
                     + pltpu.roll(zg, HEAD_DIM // 2, 1) * sb)
                o_ref[:, col0 + g * LANES:col0 + (g + 1) * LANES] = r.astype(o_ref.dtype)
        else:
            o_ref[:, col0:col0 + tn] = z.astype(o_ref.dtype)


def _norm_matmul(x, g, w, rope, *, seq, tm, tn):
    T, K = x.shape
    N = w.shape[1]
    tm = min(tm, seq)
    assert N % tn == 0 and ATT_W % tn == 0
    const = lambda i: (0, 0)
    in_specs = [
        pl.BlockSpec((tm, K), lambda i: (i, 0)),
        pl.BlockSpec((1, K), const),
        pl.BlockSpec((K, N), const, pipeline_mode=pl.Buffered(1)),
    ]
    args = [x, g, w]
    if rope is not None:
        nseq = seq // tm
        tab_spec = pl.BlockSpec((tm, LANES), lambda i: (i % nseq, 0))
        in_specs += [tab_spec] * 3
        args += list(rope)
    return pl.pallas_call(
        functools.partial(_norm_matmul_kernel, rope=rope is not None, tn=tn),
        grid=(T // tm,),
        in_specs=in_specs,
        out_specs=pl.BlockSpec((tm, N), lambda i: (i, 0)),
        out_shape=jax.ShapeDtypeStruct((T, N), BF16),
        scratch_shapes=[pltpu.VMEM((tm, K), BF16)],
        compiler_params=_cparams("parallel"),
        name="norm_matmul_rope" if rope is not None else "norm_matmul",
    )(*args)


def _attn_kernel(lamp_ref, g_ref, q_ref, k_ref, v_ref, o_ref, qt_scr, vt_scr, acc_scr, st_scr, mx_scr,
                 *, lam_init, seq, tk):
    tq = q_ref.shape[0]

    @pl.when(pl.program_id(2) == 0)
    def _():
        vt_scr[V_DIM:, :] = jnp.ones((BF16_ROWS, seq), BF16)
        for c in range(seq // tk):
            sl = slice(c * tk, (c + 1) * tk)
            vt_scr[:V_DIM, sl] = v_ref[sl, :].astype(F32).T.astype(BF16)

    qt = q_ref[...].astype(F32).T
    first_map = lax.broadcasted_iota(jnp.int32, qt.shape, 0) < HEAD_DIM
    qt_scr[0] = jnp.where(first_map, qt, 0.0).astype(BF16)
    qt_scr[1] = jnp.where(first_map, 0.0, qt).astype(BF16)
    acc_scr[...] = jnp.zeros(acc_scr.shape, F32)
    n_chunks = seq // tk

    def scores(c, slot):
        r0 = pl.multiple_of(jnp.minimum(c, n_chunks - 1) * tk, tk)
        k_c = k_ref[pl.ds(r0, tk), :]
        for mp in range(2):
            st = jnp.dot(k_c, qt_scr[mp], preferred_element_type=F32)
            st_scr[slot, mp] = st
            mx_scr[slot, mp] = jnp.max(st, axis=0, keepdims=True)

    def softmax_pv(c, slot, ms):
        r0 = pl.multiple_of(c * tk, tk)
        vt_c = vt_scr[:, pl.ds(r0, tk)]
        new_m = []
        for mp in range(2):
            m_new = jnp.maximum(ms[mp], mx_scr[slot, mp])
            alpha = jnp.exp2(ms[mp] - m_new)
            et = jnp.exp2(st_scr[slot, mp] - m_new).astype(BF16)
            acc_scr[mp] = alpha * acc_scr[mp] + jnp.dot(vt_c, et, preferred_element_type=F32)
            new_m.append(m_new)
        return tuple(new_m)

    def chunk_group(p, ms):
        c = KEY_CHUNKS_PER_ITER * p
        for i in range(KEY_CHUNKS_PER_ITER):
            scores(c + i + 1, (i + 1) % 2)
            ms = softmax_pv(c + i, i % 2, ms)
        return ms

    neg_inf = jnp.full((1, tq), -jnp.inf, F32)
    scores(0, 0)
    lax.fori_loop(0, n_chunks // KEY_CHUNKS_PER_ITER, chunk_group, (neg_inf, neg_inf))

    lp = lamp_ref[...]
    lam = (jnp.exp(jnp.sum(lp[0:1] * lp[1:2], axis=-1, keepdims=True))
           - jnp.exp(jnp.sum(lp[2:3] * lp[3:4], axis=-1, keepdims=True)) + lam_init)
    acc0, acc1 = acc_scr[0], acc_scr[1]
    ot = (acc0[:V_DIM] * (1.0 / acc0[V_DIM:V_DIM + 1])
          - acc1[:V_DIM] * (lam / acc1[V_DIM:V_DIM + 1]))
    o_ref[...] = (_rms(ot.T, g_ref[...]) * (1.0 - lam_init)).astype(o_ref.dtype)


def _attention(z, lamp, subln_g, *, batch, seq, lam_init, tq, tk):
    T = z.shape[0]
    tq, tk = min(tq, seq), min(tk, seq // KEY_CHUNKS_PER_ITER)
    assert seq % (KEY_CHUNKS_PER_ITER * tk) == 0 and seq % tq == 0
    nq = seq // tq
    return pl.pallas_call(
        functools.partial(_attn_kernel, lam_init=lam_init, seq=seq, tk=tk),
        grid=(batch, N_HEADS, nq),
        in_specs=[
            pl.BlockSpec((4, HEAD_DIM), lambda b, h, qi: (0, 0)),
            pl.BlockSpec((1, V_DIM), lambda b, h, qi: (0, 0)),
            pl.BlockSpec((tq, V_DIM), lambda b, h, qi: (b * nq + qi, COL_Q * N_HEADS + h)),
            pl.BlockSpec((seq, V_DIM), lambda b, h, qi: (b, COL_K * N_HEADS + h)),
            pl.BlockSpec((seq, V_DIM), lambda b, h, qi: (b, COL_V * N_HEADS + h)),
        ],
        out_specs=pl.BlockSpec((tq, V_DIM), lambda b, h, qi: (b * nq + qi, h)),
        out_shape=jax.ShapeDtypeStruct((T, ATT_W), BF16),
        scratch_shapes=[pltpu.VMEM((2, V_DIM, tq), BF16),
                        pltpu.VMEM((V_DIM + BF16_ROWS, seq), BF16),
                        pltpu.VMEM((2, V_DIM + BF16_ROWS, tq), F32),
                        pltpu.VMEM((2, 2, tk, tq), F32),
                        pltpu.VMEM((2, 2, 1, tq), F32)],
        compiler_params=_cparams("parallel", "parallel", "arbitrary"),
        name="diff_attention",
    )(lamp, subln_g, z, z, z)


def _softplus(y):
    return jnp.maximum(y, 0.0) + jnp.log1p(jnp.exp(-jnp.abs(y)))


def _lru_kernel(*refs, reverse, ts, ns):
    if reverse:
        (u_ref, up_ref, un_ref, cw_ref, cb_ref, wri_ref, bri_ref, lam_ref, gl_ref, hf_ref,
         o_ref, a_scr, b_scr, h_scr) = refs
    else:
        (u_ref, up_ref, un_ref, cw_ref, cb_ref, wri_ref, bri_ref, lam_ref,
         o_ref, a_scr, b_scr, h_scr) = refs
    j = pl.program_id(1)
    jj = (ns - 1 - j) if reverse else j

    @pl.when(j == 0)
    def _():
        h_scr[...] = jnp.zeros(h_scr.shape, F32)

    u = u_ref[...].astype(F32)
    up = jnp.where(jj > 0, up_ref[...].astype(F32), 0.0)
    un = jnp.where(jj < ns - 1, un_ref[...].astype(F32), 0.0)
    row8 = lax.broadcasted_iota(jnp.int32, (SUBLANES, LRU_W), 0)
    cw = cw_ref[...]
    xc = cb_ref[...] + u * cw[LRU_CONV_LEFT:LRU_CONV_LEFT + 1]
    for tap in range(cw.shape[0]):
        off = tap - LRU_CONV_LEFT
        if off == 0:
            continue
        sh = pltpu.roll(u, (-off) % ts, 0)
        if off < 0:
            edge = sh[:SUBLANES]
            for e in range(-off):
                edge = jnp.where(row8 == e, up[BF16_ROWS + off + e:BF16_ROWS + off + e + 1], edge)
            sh = jnp.concatenate([edge, sh[SUBLANES:]], axis=0)
        else:
            edge = sh[ts - SUBLANES:]
            for e in range(off):
                edge = jnp.where(row8 == SUBLANES - off + e, un[e:e + 1], edge)
            sh = jnp.concatenate([sh[:ts - SUBLANES], edge], axis=0)
        xc = xc + sh * cw[tap:tap + 1]
    xcb = xc.astype(BF16)
    c_softplus = LRU_C * _softplus(-lam_ref[...])
    for n in range(N_LRU_BLOCKS):
        sl = slice(n * LRU_BLOCK, (n + 1) * LRU_BLOCK)
        t = jnp.tanh(jnp.dot(xcb[:, sl], wri_ref[n], preferred_element_type=F32) + bri_ref[n])
        r = 0.5 * t[:, :LRU_BLOCK] + 0.5
        i = 0.5 * t[:, LRU_BLOCK:] + 0.5
        neg_log_a = r * c_softplus[:, sl]
        a = jnp.exp2(neg_log_a * (-LOG2_E))
        a_scr[:, sl] = a
        one_minus_a2 = jnp.tanh(neg_log_a) * (a * a + 1.0)
        root = one_minus_a2 * lax.rsqrt(jnp.maximum(one_minus_a2, TINY))
        b_scr[:, sl] = root * (i * xc[:, sl])

    rows = lax.broadcasted_iota(jnp.int32, (SUBLANES, LRU_W), 0)
    ng = ts // SUBLANES

    def group(g, h):
        gi = (ng - 1 - g) if reverse else g
        r0 = pl.multiple_of(gi * SUBLANES, SUBLANES)
        a8 = a_scr[pl.ds(r0, SUBLANES), :]
        b8 = b_scr[pl.ds(r0, SUBLANES), :]
        for d in (1, 2, 4):
            shift = (SUBLANES - d) if reverse else d
            keep = (rows < SUBLANES - d) if reverse else (rows >= d)
            a_sh = jnp.where(keep, pltpu.roll(a8, shift, 0), 1.0)
            b_sh = jnp.where(keep, pltpu.roll(b8, shift, 0), 0.0)
            b8 = a8 * b_sh + b8
            a8 = a8 * a_sh
        h8 = a8 * h + b8
        b_scr[pl.ds(r0, SUBLANES), :] = h8
        return h8[0:1, :] if reverse else h8[SUBLANES - 1:SUBLANES, :]

    h_scr[...] = lax.fori_loop(0, ng, group, h_scr[...])
    if reverse:
        hsum = hf_ref[...] + b_scr[...]
        o_ref[...] = (jax.nn.gelu(gl_ref[...].astype(F32)) * hsum).astype(o_ref.dtype)
    else:
        o_ref[...] = b_scr[...]


def _lru(z, conv_w, conv_b, wri, bri, lam, h_fwd, *, batch, seq, ts):
    T = z.shape[0]
    ts = min(ts, seq)
    ns = seq // ts
    reverse = h_fwd is not None
    halo_per_chunk = ts // BF16_ROWS
    last_halo = T // BF16_ROWS - 1

    def chunk(j):
        return (ns - 1 - j) if reverse else j

    def row_block(b, j):
        return b * ns + chunk(j)

    def prev_halo(b, j):
        return (jnp.maximum(row_block(b, j) * halo_per_chunk - 1, 0), COL_U)

    def next_halo(b, j):
        return (jnp.minimum((row_block(b, j) + 1) * halo_per_chunk, last_halo), COL_U)

    const = lambda b, j: (0, 0)
    in_specs = [
        pl.BlockSpec((ts, LRU_W), lambda b, j: (row_block(b, j), COL_U)),
        pl.BlockSpec((BF16_ROWS, LRU_W), prev_halo),
        pl.BlockSpec((BF16_ROWS, LRU_W), next_halo),
        pl.BlockSpec(conv_w.shape, const),
        pl.BlockSpec((1, LRU_W), const),
        pl.BlockSpec(wri.shape, lambda b, j: (0, 0, 0)),
        pl.BlockSpec(bri.shape, lambda b, j: (0, 0, 0)),
        pl.BlockSpec((1, LRU_W), const),
    ]
    args = [z, z, z, conv_w, conv_b, wri, bri, lam]
    if reverse:
        in_specs += [
            pl.BlockSpec((ts, LRU_W), lambda b, j: (row_block(b, j), COL_GL)),
            pl.BlockSpec((ts, LRU_W), lambda b, j: (row_block(b, j), 0)),
        ]
        args += [z, h_fwd]
    return pl.pallas_call(
        functools.partial(_lru_kernel, reverse=reverse, ts=ts, ns=ns),
        grid=(batch, ns),
        in_specs=in_specs,
        out_specs=pl.BlockSpec((ts, LRU_W), lambda b, j: (row_block(b, j), 0)),
        out_shape=jax.ShapeDtypeStruct((T, LRU_W), BF16 if reverse else F32),
        scratch_shapes=[pltpu.VMEM((ts, LRU_W), F32), pltpu.VMEM((ts, LRU_W), F32),
                        pltpu.VMEM((1, LRU_W), F32)],
        compiler_params=_cparams("parallel", "arbitrary"),
        name="lru_bwd" if reverse else "lru_fwd",
    )(*args)


def _merge_kernel(oa_ref, ol_ref, ga_ref, gb_ref, x_ref, wba_ref, wbl_ref, wo_ref, g_ref, o_ref):
    pa = jnp.dot(oa_ref[...], wba_ref[...], preferred_element_type=F32)
    pb = jnp.dot(ol_ref[...], wbl_ref[...], preferred_element_type=F32)
    m = (jax.nn.sigmoid(ga_ref[...].astype(F32)) * pa
         + jax.nn.sigmoid(gb_ref[...].astype(F32)) * pb)
    y = jnp.dot(m.astype(BF16), wo_ref[...], preferred_element_type=F32)
    o_ref[...] = x_ref[...] + _rms(y, g_ref[...])


def _merge(o_attn, o_lru, z, x, w_ba, w_bl, w_out, g, *, seq, tm):
    T = x.shape[0]
    tm = min(tm, seq)
    row = lambda i: (i, 0)
    const = lambda i: (0, 0)
    wspec = pl.BlockSpec((D_MODEL, D_MODEL), const)
    return pl.pallas_call(
        _merge_kernel,
        grid=(T // tm,),
        in_specs=[
            pl.BlockSpec((tm, ATT_W), row),
            pl.BlockSpec((tm, LRU_W), row),
            pl.BlockSpec((tm, D_MODEL), lambda i: (i, COL_GA)),
            pl.BlockSpec((tm, D_MODEL), lambda i: (i, COL_GB)),
            pl.BlockSpec((tm, D_MODEL), row),
            wspec, wspec, wspec,
            pl.BlockSpec((1, D_MODEL), const),
        ],
        out_specs=pl.BlockSpec((tm, D_MODEL), row),
        out_shape=jax.ShapeDtypeStruct((T, D_MODEL), F32),
        compiler_params=_cparams("parallel"),
        name="merge_out",
    )(o_attn, o_lru, z, z, x, w_ba, w_bl, w_out, g)


def _ffn_down_kernel(c_ref, cp_ref, cn_ref, lin_ref, x_ref, p_ref, cw_ref, cb_ref, wd_ref, g_ref,
                     wg_ref, wp_ref, o_ref, act_scr, *, tm, sub, nseq):
    pos = pl.program_id(0) % nseq
    row8 = lax.broadcasted_iota(jnp.int32, (SUBLANES, D_FF), 0)
    cw = cw_ref[...]
    for s in range(tm // sub):
        r0 = s * sub
        c = c_ref[r0:r0 + sub, :].astype(F32)
        if s == 0:
            before = jnp.where(pos > 0, cp_ref[BF16_ROWS - 1:BF16_ROWS, :].astype(F32), 0.0)
        else:
            before = c_ref[r0 - BF16_ROWS:r0, :][BF16_ROWS - 1:BF16_ROWS, :].astype(F32)
        if r0 + sub == tm:
            after = jnp.where(pos < nseq - 1, cn_ref[0:1, :].astype(F32), 0.0)
        else:
            after = c_ref[r0 + sub:r0 + sub + BF16_ROWS, :][0:1, :].astype(F32)
        prev = pltpu.roll(c, 1, 0)
        prev = jnp.concatenate([jnp.where(row8 == 0, before, prev[:SUBLANES]), prev[SUBLANES:]], axis=0)
        nxt = pltpu.roll(c, sub - 1, 0)
        nxt = jnp.concatenate([nxt[:sub - SUBLANES],
                               jnp.where(row8 == SUBLANES - 1, after, nxt[sub - SUBLANES:])], axis=0)
        conv = prev * cw[0:1] + c * cw[1:2] + nxt * cw[2:3] + cb_ref[...]
        inner = conv * (GELU_K0 + GELU_K1 * (conv * conv))
        gate2 = 1.0 + jnp.tanh(inner)
        act_scr[r0:r0 + sub, :] = ((conv * lin_ref[r0:r0 + sub, :].astype(F32)) * gate2).astype(BF16)
    for s in range(tm // sub):
        rows = slice(s * sub, (s + 1) * sub)
        f = jnp.dot(act_scr[rows, :], wd_ref[...], preferred_element_type=F32)
        x = x_ref[rows, :] + _rms(f, g_ref[...])
        gate = jax.nn.sigmoid(jnp.dot(x.astype(BF16), wg_ref[...], preferred_element_type=F32))
        emb = jnp.dot(p_ref[rows, :].astype(BF16), wp_ref[...], preferred_element_type=F32)
        o_ref[rows, :] = x + gate * emb


def _ffn_down(cl, x, p, conv_w, conv_b, w_down, g, w_gate, w_proj, *, seq, tm):
    T = x.shape[0]
    tm = min(tm, seq)
    sub = min(FFN_SUB_ROWS, tm)
    nseq = seq // tm
    halo_per_tile = tm // BF16_ROWS
    last_halo = T // BF16_ROWS - 1
    row = lambda i: (i, 0)
    const = lambda i: (0, 0)
    return pl.pallas_call(
        functools.partial(_ffn_down_kernel, tm=tm, sub=sub, nseq=nseq),
        grid=(T // tm,),
        in_specs=[
            pl.BlockSpec((tm, D_FF), row),
            pl.BlockSpec((BF16_ROWS, D_FF), lambda i: (jnp.maximum(i * halo_per_tile - 1, 0), 0)),
            pl.BlockSpec((BF16_ROWS, D_FF),
                         lambda i: (jnp.minimum((i + 1) * halo_per_tile, last_halo), 0)),
            pl.BlockSpec((tm, D_FF), lambda i: (i, 1)),
            pl.BlockSpec((tm, D_MODEL), row),
            pl.BlockSpec((tm, PLE_DIM), row),
            pl.BlockSpec(conv_w.shape, const),
            pl.BlockSpec((1, D_FF), const),
            pl.BlockSpec((D_FF, D_MODEL), const, pipeline_mode=pl.Buffered(1)),
            pl.BlockSpec((1, D_MODEL), const),
            pl.BlockSpec((D_MODEL, D_MODEL), const, pipeline_mode=pl.Buffered(1)),
            pl.BlockSpec((PLE_DIM, D_MODEL), const, pipeline_mode=pl.Buffered(1)),
        ],
        out_specs=pl.BlockSpec((tm, D_MODEL), row),
        out_shape=jax.ShapeDtypeStruct((T, D_MODEL), F32),
        scratch_shapes=[pltpu.VMEM((tm, D_FF), BF16)],
        compiler_params=_cparams("parallel"),
        name="ffn_down_ple",
    )(cl, cl, cl, cl, x, p, conv_w, conv_b, w_down, g, w_gate, w_proj)


def _rope_tables(seq):
    half = HEAD_DIM // 2
    inv = ROPE_THETA ** (-jnp.arange(0, HEAD_DIM, 2, dtype=F32) / HEAD_DIM)
    ang = jnp.arange(seq, dtype=F32)[:, None] * inv[None, :]
    lane = jnp.arange(LANES)
    cos = jnp.cos(ang)[:, lane % half]
    sin = jnp.sin(ang)[:, lane % half]
    lower = (lane % HEAD_DIM) < half
    return cos, jnp.where(lower, -sin, 0.0), jnp.where(lower, 0.0, sin)


def _prep_layer(i, w):
    q_scale = jnp.where(jnp.arange(IN_COLS) < ATT_W, HEAD_DIM ** -0.5, 1.0).astype(F32)
    row = lambda a: a.reshape(1, -1).astype(F32)
    wri = (0.5 * jnp.concatenate([w["rg_wa"][i], w["rg_wx"][i]], axis=-1)).astype(BF16)
    bri = 0.5 * jnp.concatenate([w["rg_ba"][i].reshape(2, N_LRU_BLOCKS, 1, LRU_BLOCK),
                                 w["rg_bx"][i].reshape(2, N_LRU_BLOCKS, 1, LRU_BLOCK)], axis=-1)
    return dict(
        norm_mix_pre=row(w["norm_mix_pre"][i]),
        norm_mix_post=row(w["norm_mix_post"][i]),
        w_in=(w["w_in"][i] * q_scale).astype(BF16),
        lamp=jnp.stack([w["lam_q1"][i], w["lam_k1"][i], w["lam_q2"][i], w["lam_k2"][i]]).astype(F32),
        subln_g=row(w["subln_g"][i]),
        lru_conv_w=w["lru_conv_w"][i].astype(F32),
        lru_conv_b=row(w["lru_conv_b"][i]),
        wri=wri, bri=bri.astype(F32),
        rg_lambda=w["rg_lambda"][i].reshape(2, 1, LRU_W).astype(F32),
        w_branch_attn=w["w_branch_attn"][i].astype(BF16),
        w_branch_lru=w["w_branch_lru"][i].astype(BF16),
        w_out=w["w_out"][i].astype(BF16),
        norm_ffn_pre=row(w["norm_ffn_pre"][i]),
        norm_ffn_post=row(w["norm_ffn_post"][i]),
        w_ffn_up=w["w_ffn_up"][i].astype(BF16),
        ffn_conv_w=w["ffn_conv_w"][i].astype(F32),
        ffn_conv_b=row(w["ffn_conv_b"][i]),
        w_ffn_down=(0.5 * w["w_ffn_down"][i]).astype(BF16),
        w_ple_proj=w["w_ple_proj"][i].astype(BF16),
        w_ple_gate=w["w_ple_gate"][i].astype(BF16),
    )


def _layer(x, p, lp, rope, lam_init, *, batch, seq, tiles):
    z = _norm_matmul(x, lp["norm_mix_pre"], lp["w_in"], rope, seq=seq, tm=tiles["tm"], tn=tiles["tn"])
    o_attn = _attention(z, lp["lamp"], lp["subln_g"], batch=batch, seq=seq, lam_init=lam_init,
                        tq=tiles["tq"], tk=tiles["tk"])
    h_fwd = _lru(z, lp["lru_conv_w"], lp["lru_conv_b"], lp["wri"][0], lp["bri"][0], lp["rg_lambda"][0],
                 None, batch=batch, seq=seq, ts=tiles["ts"])
    o_lru = _lru(z, lp["lru_conv_w"], lp["lru_conv_b"], lp["wri"][1], lp["bri"][1], lp["rg_lambda"][1],
                 h_fwd, batch=batch, seq=seq, ts=tiles["ts"])
    x1 = _merge(o_attn, o_lru, z, x, lp["w_branch_attn"], lp["w_branch_lru"], lp["w_out"],
                lp["norm_mix_post"], seq=seq, tm=tiles["tm"])
    cl = _norm_matmul(x1, lp["norm_ffn_pre"], lp["w_ffn_up"], None, seq=seq, tm=tiles["tm"], tn=tiles["tn"])
    return _ffn_down(cl, x1, p, lp["ffn_conv_w"], lp["ffn_conv_b"], lp["w_ffn_down"], lp["norm_ffn_post"],
                     lp["w_ple_gate"], lp["w_ple_proj"], seq=seq, tm=tiles["tm_ffn"])


TILES = dict(tm=512, tn=1024, tq=512, tk=512, ts=512, tm_ffn=512)


def _trunk(x, p, layers, tiles):
    batch, seq, _ = x.shape
    rope = _rope_tables(seq)
    h = x.reshape(batch * seq, D_MODEL)
    for i, lp in enumerate(layers):
        lam_init = 0.8 - 0.6 * math.exp(-0.3 * i)
        h = _layer(h, p[i].reshape(batch * seq, PLE_DIM), lp, rope, lam_init,
                   batch=batch, seq=seq, tiles=tiles)
    return h.reshape(batch, seq, D_MODEL)


def kernel(x_prompt, x_sample, p_prompt, p_sample, norm_mix_pre, norm_mix_post, w_in, lam_q1, lam_k1, lam_q2, lam_k2, subln_g, lru_conv_w, lru_conv_b, rg_wa, rg_ba, rg_wx, rg_bx, rg_lambda, w_branch_attn, w_branch_lru, w_out, norm_ffn_pre, norm_ffn_post, w_ffn_up, ffn_conv_w, ffn_conv_b, w_ffn_down, w_ple_proj, w_ple_gate):
    w = dict(norm_mix_pre=norm_mix_pre, norm_mix_post=norm_mix_post, w_in=w_in, lam_q1=lam_q1,
             lam_k1=lam_k1, lam_q2=lam_q2, lam_k2=lam_k2, subln_g=subln_g, lru_conv_w=lru_conv_w,
             lru_conv_b=lru_conv_b, rg_wa=rg_wa, rg_ba=rg_ba, rg_wx=rg_wx, rg_bx=rg_bx,
             rg_lambda=rg_lambda, w_branch_attn=w_branch_attn, w_branch_lru=w_branch_lru, w_out=w_out,
             norm_ffn_pre=norm_ffn_pre, norm_ffn_post=norm_ffn_post, w_ffn_up=w_ffn_up,
             ffn_conv_w=ffn_conv_w, ffn_conv_b=ffn_conv_b, w_ffn_down=w_ffn_down,
             w_ple_proj=w_ple_proj, w_ple_gate=w_ple_gate)
    layers = [_prep_layer(i, w) for i in range(w_in.shape[0])]
    return (_trunk(x_prompt, p_prompt, layers, TILES), _trunk(x_sample, p_sample, layers, TILES))
```

```python
import functools
import math

import jax
import jax.numpy as jnp
from jax import lax
from jax.experimental import pallas as pl
from jax.experimental.pallas import tpu as pltpu

F32 = jnp.float32
BF16 = jnp.bfloat16

D_MODEL = 1024
N_HEADS = 8
HEAD_DIM = 64
V_DIM = 2 * HEAD_DIM
ATT_W = N_HEADS * V_DIM
LRU_W = 1024
N_LRU_BLOCKS = 8
LRU_BLOCK = LRU_W // N_LRU_BLOCKS
LRU_CONV_LEFT = 2
LRU_C = 8.0
D_FF = 3072
PLE_DIM = 256
EPS = 1e-6
TINY = 1e-30
ROPE_THETA = 10000.0
IN_COLS = 3 * ATT_W + 2 * LRU_W + 2 * D_MODEL
LOG2_E = math.log2(math.e)
KEY_CHUNKS_PER_ITER = 4
FFN_CONV_LEFT = 1
HALO = 16
GELU_K0 = math.sqrt(2.0 / math.pi)
GELU_K1 = 0.044715 * GELU_K0

LANES = 128
SUBLANES = 8
BF16_ROWS = 16
VMEM_LIMIT_BYTES = 56 * 1024 * 1024

COL_Q, COL_K, COL_V, COL_U, COL_GL, COL_GA, COL_GB = range(7)


def _cparams(*sem):
    return pltpu.CompilerParams(dimension_semantics=sem, vmem_limit_bytes=VMEM_LIMIT_BYTES)


def _rms(x, g):
    return x * lax.rsqrt(jnp.mean(x * x, axis=-1, keepdims=True) + EPS) * g


def _fill_normed(x_ref, xp_ref, xn_ref, g_ref, h_scr, *, tm, nseq):
    pos = pl.program_id(0) % nseq
    g = g_ref[...]
    h_scr[:HALO, :] = jnp.where(pos > 0, _rms(xp_ref[...], g), 0.0).astype(BF16)
    h_scr[HALO:HALO + tm, :] = _rms(x_ref[...], g).astype(BF16)
    h_scr[HALO + tm:, :] = jnp.where(pos < nseq - 1, _rms(xn_ref[...], g), 0.0).astype(BF16)


def _dwconv_rows(z_ext, cw, cb, *, left, tm):
    rows = z_ext.shape[0]
    out = cb
    for tap in range(cw.shape[0]):
        off = tap - left
        shifted = z_ext if off == 0 else pltpu.roll(z_ext, (-off) % rows, 0)
        out = out + shifted[HALO:HALO + tm] * cw[tap:tap + 1]
    return out


def _gelu2(x):
    return x * (1.0 + jnp.tanh(x * (GELU_K0 + GELU_K1 * (x * x))))


def _in_proj_kernel(x_ref, xp_ref, xn_ref, g_ref, w_ref, cos_ref, sa_ref, sb_ref, cw_ref, cb_ref,
                    o_ref, h_scr, *, tm, tn, nseq):
    _fill_normed(x_ref, xp_ref, xn_ref, g_ref, h_scr, tm=tm, nseq=nseq)
    for j in range(w_ref.shape[1] // tn):
        col0 = j * tn
        cols = slice(col0, col0 + tn)
        col_block = col0 // D_MODEL
        if col_block == COL_U:
            z_ext = jnp.dot(h_scr[...], w_ref[:, cols], preferred_element_type=F32)
            u0 = col0 - COL_U * D_MODEL
            xc = _dwconv_rows(z_ext, cw_ref[:, u0:u0 + tn], cb_ref[:, u0:u0 + tn],
                              left=LRU_CONV_LEFT, tm=tm)
            o_ref[:, cols] = xc.astype(o_ref.dtype)
            continue
        z = jnp.dot(h_scr[HALO:HALO + tm, :], w_ref[:, cols], preferred_element_type=F32)
        if col_block in (COL_Q, COL_K):
            scale = LOG2_E if col_block == COL_Q else 1.0
            cos, sa, sb = cos_ref[...] * scale, sa_ref[...] * scale, sb_ref[...] * scale
            for g in range(tn // LANES):
                zg = z[:, g * LANES:(g + 1) * LANES]
                r = (zg * cos + pltpu.roll(zg, LANES - HEAD_DIM // 2, 1) * sa
                     + pltpu.roll(zg, HEAD_DIM // 2, 1) * sb)
                o_ref[:, col0 + g * LANES:col0 + (g + 1) * LANES] = r.astype(o_ref.dtype)
        elif col_block == COL_GL:
            o_ref[:, cols] = (0.5 * _gelu2(z)).astype(o_ref.dtype)
        else:
            o_ref[:, cols] = z.astype(o_ref.dtype)


def _ffn_up_kernel(x_ref, xp_ref, xn_ref, g_ref, w_ref, cw_ref, cb_ref, o_ref, h_scr, *, tm, tn, nseq):
    _fill_normed(x_ref, xp_ref, xn_ref, g_ref, h_scr, tm=tm, nseq=nseq)
    for j in range(D_FF // tn):
        cols = slice(j * tn, (j + 1) * tn)
        c_ext = jnp.dot(h_scr[...], w_ref[:, cols], preferred_element_type=F32)
        lin = jnp.dot(h_scr[HALO:HALO + tm, :], w_ref[:, D_FF + j * tn:D_FF + (j + 1) * tn],
                      preferred_element_type=F32)
        conv = _dwconv_rows(c_ext, cw_ref[:, cols], cb_ref[:, cols], left=FFN_CONV_LEFT, tm=tm)
        o_ref[:, cols] = (_gelu2(conv) * lin).astype(o_ref.dtype)


def _norm_matmul(kernel_fn, x, g, w, extra, n_out, *, seq, tm, tn, name):
    T, K = x.shape
    tm = min(tm, seq)
    nseq = seq // tm
    assert w.shape[1] % tn == 0 and D_MODEL % tn == 0 and tm % HALO == 0
    halo_per_tile = tm // HALO
    last_halo = T // HALO - 1
    const = lambda i: (0, 0)
    in_specs = [
        pl.BlockSpec((tm, K), lambda i: (i, 0)),
        pl.BlockSpec((HALO, K), lambda i: (jnp.maximum(i * halo_per_tile - 1, 0), 0)),
        pl.BlockSpec((HALO, K), lambda i: (jnp.minimum((i + 1) * halo_per_tile, last_halo), 0)),
        pl.BlockSpec((1, K), const),
        pl.BlockSpec(w.shape, const, pipeline_mode=pl.Buffered(1)),
    ] + [spec for _, spec in extra]
    return pl.pallas_call(
        functools.partial(kernel_fn, tm=tm, tn=tn, nseq=nseq),
        grid=(T // tm,),
        in_specs=in_specs,
        out_specs=pl.BlockSpec((tm, n_out), lambda i: (i, 0)),
        out_shape=jax.ShapeDtypeStruct((T, n_out), BF16),
        scratch_shapes=[pltpu.VMEM((tm + 2 * HALO, K), BF16)],
        compiler_params=_cparams("parallel"),
        name=name,
    )(x, x, x, g, w, *[a for a, _ in extra])


def _in_proj(x, g, w, rope, conv_w, conv_b, *, seq, tm, tn):
    tm = min(tm, seq)
    nseq = seq // tm
    const = lambda i: (0, 0)
    tab_spec = pl.BlockSpec((tm, LANES), lambda i: (i % nseq, 0))
    extra = [(t, tab_spec) for t in rope]
    extra += [(conv_w, pl.BlockSpec(conv_w.shape, const)), (conv_b, pl.BlockSpec(conv_b.shape, const))]
    return _norm_matmul(_in_proj_kernel, x, g, w, extra, IN_COLS, seq=seq, tm=tm, tn=tn, name="in_proj")


def _ffn_up(x, g, w, conv_w, conv_b, *, seq, tm, tn):
    const = lambda i: (0, 0)
    extra = [(conv_w, pl.BlockSpec(conv_w.shape, const)), (conv_b, pl.BlockSpec(conv_b.shape, const))]
    return _norm_matmul(_ffn_up_kernel, x, g, w, extra, D_FF, seq=seq, tm=tm, tn=tn, name="ffn_up")


def _attn_kernel(lamp_ref, g_ref, q_ref, k_ref, v_ref, o_ref, qt_scr, vt_scr, acc_scr, st_scr, mx_scr,
                 *, lam_init, seq, tq, tk, group):
    nq = seq // tq
    n_chunks = seq // tk
    n_groups = n_chunks // group

    vt_scr[V_DIM:, :] = jnp.ones((BF16_ROWS, seq), BF16)
    for c in range(n_chunks):
        sl = slice(c * tk, (c + 1) * tk)
        vt_scr[:V_DIM, sl] = v_ref[sl, :].astype(F32).T.astype(BF16)

    lp = lamp_ref[...]
    lam = (jnp.exp(jnp.sum(lp[0:1] * lp[1:2], axis=-1, keepdims=True))
           - jnp.exp(jnp.sum(lp[2:3] * lp[3:4], axis=-1, keepdims=True)) + lam_init)
    first_map = lax.broadcasted_iota(jnp.int32, (V_DIM, tq), 0) < HEAD_DIM

    def prep_q(qi, qslot):
        r0 = pl.multiple_of(jnp.minimum(qi, nq - 1) * tq, tq)
        qt = q_ref[pl.ds(r0, tq), :].astype(F32).T
        qt_scr[qslot, 0] = jnp.where(first_map, qt, 0.0).astype(BF16)
        qt_scr[qslot, 1] = jnp.where(first_map, 0.0, qt).astype(BF16)

    def scores(c, slot, qslot):
        r0 = pl.multiple_of(c * tk, tk)
        k_c = k_ref[pl.ds(r0, tk), :]
        for mp in range(2):
            st = jnp.dot(k_c, qt_scr[qslot, mp], preferred_element_type=F32)
            st_scr[slot, mp] = st
            mx_scr[slot, mp] = jnp.max(st, axis=0, keepdims=True)

    def softmax_pv(c, slot, ms):
        r0 = pl.multiple_of(c * tk, tk)
        vt_c = vt_scr[:, pl.ds(r0, tk)]
        new_m = []
        for mp in range(2):
            m_new = jnp.maximum(ms[mp], mx_scr[slot, mp])
            alpha = jnp.exp2(ms[mp] - m_new)
            et = jnp.exp2(st_scr[slot, mp] - m_new).astype(BF16)
            acc_scr[mp] = alpha * acc_scr[mp] + jnp.dot(vt_c, et, preferred_element_type=F32)
            new_m.append(m_new)
        return tuple(new_m)

    def chunk_group(c0, ms, qslot, last):
        for i in range(group):
            if last and i == group - 1:
                scores(0, 0, 1 - qslot)
            else:
                scores(c0 + i + 1, (i + 1) % 2, qslot)
            ms = softmax_pv(c0 + i, i % 2, ms)
        return ms

    def query_tile(qi, carry):
        qslot = qi % 2
        prep_q(qi + 1, 1 - qslot)
        acc_scr[...] = jnp.zeros(acc_scr.shape, F32)
        neg_inf = jnp.full((1, tq), -jnp.inf, F32)
        ms = lax.fori_loop(
            0, n_groups - 1,
            lambda p, ms: chunk_group(group * p, ms, qslot, False),
            (neg_inf, neg_inf))
        chunk_group(group * (n_groups - 1), ms, qslot, True)
        acc0, acc1 = acc_scr[0], acc_scr[1]
        ot = (acc0[:V_DIM] * (1.0 / acc0[V_DIM:V_DIM + 1])
              - acc1[:V_DIM] * (lam / acc1[V_DIM:V_DIM + 1]))
        r0 = pl.multiple_of(qi * tq, tq)
        o_ref[pl.ds(r0, tq), :] = (_rms(ot.T, g_ref[...]) * (1.0 - lam_init)).astype(o_ref.dtype)
        return carry

    prep_q(0, 0)
    scores(0, 0, 0)
    lax.fori_loop(0, nq, query_tile, 0)


def _attention(z, lamp, subln_g, *, batch, seq, lam_init, tq, tk):
    T = z.shape[0]
    tq, tk = min(tq, seq), min(tk, seq // 2)
    group = min(KEY_CHUNKS_PER_ITER, seq // tk)
    assert group % 2 == 0 and seq % (group * tk) == 0 and seq % tq == 0
    seq_block = lambda col: pl.BlockSpec((seq, V_DIM), lambda b, h: (b, col * N_HEADS + h))
    return pl.pallas_call(
        functools.partial(_attn_kernel, lam_init=lam_init, seq=seq, tq=tq, tk=tk, group=group),
        grid=(batch, N_HEADS),
        in_specs=[
            pl.BlockSpec((4, HEAD_DIM), lambda b, h: (0, 0)),
            pl.BlockSpec((1, V_DIM), lambda b, h: (0, 0)),
            seq_block(COL_Q), seq_block(COL_K), seq_block(COL_V),
        ],
        out_specs=pl.BlockSpec((seq, V_DIM), lambda b, h: (b, h)),
        out_shape=jax.ShapeDtypeStruct((T, ATT_W), BF16),
        scratch_shapes=[pltpu.VMEM((2, 2, V_DIM, tq), BF16),
                        pltpu.VMEM((V_DIM + BF16_ROWS, seq), BF16),
                        pltpu.VMEM((2, V_DIM + BF16_ROWS, tq), F32),
                        pltpu.VMEM((2, 2, tk, tq), F32),
                        pltpu.VMEM((2, 2, 1, tq), F32)],
        compiler_params=_cparams("parallel", "parallel"),
        name="diff_attention",
    )(lamp, subln_g, z, z, z)


def _softplus(y):
    return jnp.maximum(y, 0.0) + jnp.log1p(jnp.exp(-jnp.abs(y)))


def _lru_kernel(*refs, reverse, ts):
    if reverse:
        xc_ref, wri_ref, bri_ref, lam_ref, gl_ref, hf_ref, o_ref, a_scr, b_scr, h_scr = refs
    else:
        xc_ref, wri_ref, bri_ref, lam_ref, o_ref, a_scr, b_scr, h_scr = refs

    @pl.when(pl.program_id(1) == 0)
    def _():
        h_scr[...] = jnp.zeros(h_scr.shape, F32)

    xcb = xc_ref[...]
    xc = xcb.astype(F32)
    c_softplus = LRU_C * _softplus(-lam_ref[...])
    for n in range(N_LRU_BLOCKS):
        sl = slice(n * LRU_BLOCK, (n + 1) * LRU_BLOCK)
        t = jnp.tanh(jnp.dot(xcb[:, sl], wri_ref[n], preferred_element_type=F32) + bri_ref[n])
        r = 0.5 * t[:, :LRU_BLOCK] + 0.5
        i = 0.5 * t[:, LRU_BLOCK:] + 0.5
        neg_log_a = r * c_softplus[:, sl]
        a = jnp.exp2(neg_log_a * (-LOG2_E))
        a_scr[:, sl] = a
        one_minus_a2 = jnp.tanh(neg_log_a) * (a * a + 1.0)
        root = one_minus_a2 * lax.rsqrt(jnp.maximum(one_minus_a2, TINY))
        b_scr[:, sl] = root * (i * xc[:, sl])

    rows = lax.broadcasted_iota(jnp.int32, (SUBLANES, LRU_W), 0)
    ng = ts // SUBLANES

    def group(g, h):
        gi = (ng - 1 - g) if reverse else g
        r0 = pl.multiple_of(gi * SUBLANES, SUBLANES)
        a8 = a_scr[pl.ds(r0, SUBLANES), :]
        b8 = b_scr[pl.ds(r0, SUBLANES), :]
        for d in (1, 2, 4):
            shift = (SUBLANES - d) if reverse else d
            keep = (rows < SUBLANES - d) if reverse else (rows >= d)
            a_sh = jnp.where(keep, pltpu.roll(a8, shift, 0), 1.0)
            b_sh = jnp.where(keep, pltpu.roll(b8, shift, 0), 0.0)
            b8 = a8 * b_sh + b8
            a8 = a8 * a_sh
        h8 = a8 * h + b8
        b_scr[pl.ds(r0, SUBLANES), :] = h8
        return h8[0:1, :] if reverse else h8[SUBLANES - 1:SUBLANES, :]

    h_scr[...] = lax.fori_loop(0, ng, group, h_scr[...])
    if reverse:
        o_ref[...] = (gl_ref[...].astype(F32) * (hf_ref[...] + b_scr[...])).astype(o_ref.dtype)
    else:
        o_ref[...] = b_scr[...]


def _lru(z, wri, bri, lam, h_fwd, *, batch, seq, ts):
    T = z.shape[0]
    ts = min(ts, seq)
    ns = seq // ts
    reverse = h_fwd is not None

    def row_block(b, j):
        return b * ns + ((ns - 1 - j) if reverse else j)

    in_specs = [
        pl.BlockSpec((ts, LRU_W), lambda b, j: (row_block(b, j), COL_U)),
        pl.BlockSpec(wri.shape, lambda b, j: (0, 0, 0)),
        pl.BlockSpec(bri.shape, lambda b, j: (0, 0, 0)),
        pl.BlockSpec((1, LRU_W), lambda b, j: (0, 0)),
    ]
    args = [z, wri, bri, lam]
    if reverse:
        in_specs += [
            pl.BlockSpec((ts, LRU_W), lambda b, j: (row_block(b, j), COL_GL)),
            pl.BlockSpec((ts, LRU_W), lambda b, j: (row_block(b, j), 0)),
        ]
        args += [z, h_fwd]
    return pl.pallas_call(
        functools.partial(_lru_kernel, reverse=reverse, ts=ts),
        grid=(batch, ns),
        in_specs=in_specs,
        out_specs=pl.BlockSpec((ts, LRU_W), lambda b, j: (row_block(b, j), 0)),
        out_shape=jax.ShapeDtypeStruct((T, LRU_W), BF16 if reverse else F32),
        scratch_shapes=[pltpu.VMEM((ts, LRU_W), F32), pltpu.VMEM((ts, LRU_W), F32),
                        pltpu.VMEM((1, LRU_W), F32)],
        compiler_params=_cparams("parallel", "arbitrary"),
        name="lru_bwd" if reverse else "lru_fwd",
    )(*args)


def _merge_kernel(oa_ref, ol_ref, ga_ref, gb_ref, x_ref, wba_ref, wbl_ref, wo_ref, g_ref, o_ref):
    pa = jnp.dot(oa_ref[...], wba_ref[...], preferred_element_type=F32)
    pb = jnp.dot(ol_ref[...], wbl_ref[...], preferred_element_type=F32)
    m = (jax.nn.sigmoid(ga_ref[...].astype(F32)) * pa
         + jax.nn.sigmoid(gb_ref[...].astype(F32)) * pb)
    y = jnp.dot(m.astype(BF16), wo_ref[...], preferred_element_type=F32)
    o_ref[...] = x_ref[...] + _rms(y, g_ref[...])


def _merge(o_attn, o_lru, z, x, w_ba, w_bl, w_out, g, *, seq, tm):
    T = x.shape[0]
    tm = min(tm, seq)
    row = lambda i: (i, 0)
    const = lambda i: (0, 0)
    wspec = pl.BlockSpec((D_MODEL, D_MODEL), const)
    return pl.pallas_call(
        _merge_kernel,
        grid=(T // tm,),
        in_specs=[
            pl.BlockSpec((tm, ATT_W), row),
            pl.BlockSpec((tm, LRU_W), row),
            pl.BlockSpec((tm, D_MODEL), lambda i: (i, COL_GA)),
            pl.BlockSpec((tm, D_MODEL), lambda i: (i, COL_GB)),
            pl.BlockSpec((tm, D_MODEL), row),
            wspec, wspec, wspec,
            pl.BlockSpec((1, D_MODEL), const),
        ],
        out_specs=pl.BlockSpec((tm, D_MODEL), row),
        out_shape=jax.ShapeDtypeStruct((T, D_MODEL), F32),
        compiler_params=_cparams("parallel"),
        name="merge_out",
    )(o_attn, o_lru, z, z, x, w_ba, w_bl, w_out, g)


def _ffn_down_kernel(act_ref, x_ref, p_ref, wd_ref, g_ref, wg_ref, wp_ref, o_ref):
    f = jnp.dot(act_ref[...], wd_ref[...], preferred_element_type=F32)
    x = x_ref[...] + _rms(f, g_ref[...])
    gate = jax.nn.sigmoid(jnp.dot(x.astype(BF16), wg_ref[...], preferred_element_type=F32))
    emb = jnp.dot(p_ref[...].astype(BF16), wp_ref[...], preferred_element_type=F32)
    o_ref[...] = x + gate * emb


def _ffn_down(act, x, p, w_down, g, w_gate, w_proj, *, seq, tm):
    T = x.shape[0]
    tm = min(tm, seq)
    row = lambda i: (i, 0)
    const = lambda i: (0, 0)
    return pl.pallas_call(
        _ffn_down_kernel,
        grid=(T // tm,),
        in_specs=[
            pl.BlockSpec((tm, D_FF), row),
            pl.BlockSpec((tm, D_MODEL), row),
            pl.BlockSpec((tm, PLE_DIM), row),
            pl.BlockSpec((D_FF, D_MODEL), const, pipeline_mode=pl.Buffered(1)),
            pl.BlockSpec((1, D_MODEL), const),
            pl.BlockSpec((D_MODEL, D_MODEL), const, pipeline_mode=pl.Buffered(1)),
            pl.BlockSpec((PLE_DIM, D_MODEL), const, pipeline_mode=pl.Buffered(1)),
        ],
        out_specs=pl.BlockSpec((tm, D_MODEL), row),
        out_shape=jax.ShapeDtypeStruct((T, D_MODEL), F32),
        compiler_params=_cparams("parallel"),
        name="ffn_down_ple",
    )(act, x, p, w_down, g, w_gate, w_proj)


def _rope_tables(seq):
    half = HEAD_DIM // 2
    inv = ROPE_THETA ** (-jnp.arange(0, HEAD_DIM, 2, dtype=F32) / HEAD_DIM)
    ang = jnp.arange(seq, dtype=F32)[:, None] * inv[None, :]
    lane = jnp.arange(LANES)
    cos = jnp.cos(ang)[:, lane % half]
    sin = jnp.sin(ang)[:, lane % half]
    lower = (lane % HEAD_DIM) < half
    return cos, jnp.where(lower, -sin, 0.0), jnp.where(lower, 0.0, sin)


def _prep_layer(i, w):
    q_scale = jnp.where(jnp.arange(IN_COLS) < ATT_W, HEAD_DIM ** -0.5, 1.0).astype(F32)
    row = lambda a: a.reshape(1, -1).astype(F32)
    wri = (0.5 * jnp.concatenate([w["rg_wa"][i], w["rg_wx"][i]], axis=-1)).astype(BF16)
    bri = 0.5 * jnp.concatenate([w["rg_ba"][i].reshape(2, N_LRU_BLOCKS, 1, LRU_BLOCK),
                                 w["rg_bx"][i].reshape(2, N_LRU_BLOCKS, 1, LRU_BLOCK)], axis=-1)
    return dict(
        norm_mix_pre=row(w["norm_mix_pre"][i]),
        norm_mix_post=row(w["norm_mix_post"][i]),
        w_in=(w["w_in"][i] * q_scale).astype(BF16),
        lamp=jnp.stack([w["lam_q1"][i], w["lam_k1"][i], w["lam_q2"][i], w["lam_k2"][i]]).astype(F32),
        subln_g=row(w["subln_g"][i]),
        lru_conv_w=w["lru_conv_w"][i].astype(F32),
        lru_conv_b=row(w["lru_conv_b"][i]),
        wri=wri, bri=bri.astype(F32),
        rg_lambda=w["rg_lambda"][i].reshape(2, 1, LRU_W).astype(F32),
        w_branch_attn=w["w_branch_attn"][i].astype(BF16),
        w_branch_lru=w["w_branch_lru"][i].astype(BF16),
        w_out=w["w_out"][i].astype(BF16),
        norm_ffn_pre=row(w["norm_ffn_pre"][i]),
        norm_ffn_post=row(w["norm_ffn_post"][i]),
        w_ffn_up=w["w_ffn_up"][i].astype(BF16),
        ffn_conv_w=w["ffn_conv_w"][i].astype(F32),
        ffn_conv_b=row(w["ffn_conv_b"][i]),
        w_ffn_down=(0.5 * w["w_ffn_down"][i]).astype(BF16),
        w_ple_proj=w["w_ple_proj"][i].astype(BF16),
        w_ple_gate=w["w_ple_gate"][i].astype(BF16),
    )


def _layer(x, p, lp, rope, lam_init, *, batch, seq, tiles):
    z = _in_proj(x, lp["norm_mix_pre"], lp["w_in"], rope, lp["lru_conv_w"], lp["lru_conv_b"],
                 seq=seq, tm=tiles["tm"], tn=tiles["tn"])
    o_attn = _attention(z, lp["lamp"], lp["subln_g"], batch=batch, seq=seq, lam_init=lam_init,
                        tq=tiles["tq"], tk=tiles["tk"])
    h_fwd = _lru(z, lp["wri"][0], lp["bri"][0], lp["rg_lambda"][0], None,
                 batch=batch, seq=seq, ts=tiles["ts"])
    o_lru = _lru(z, lp["wri"][1], lp["bri"][1], lp["rg_lambda"][1], h_fwd,
                 batch=batch, seq=seq, ts=tiles["ts"])
    x1 = _merge(o_attn, o_lru, z, x, lp["w_branch_attn"], lp["w_branch_lru"], lp["w_out"],
                lp["norm_mix_post"], seq=seq, tm=tiles["tm"])
    act = _ffn_up(x1, lp["norm_ffn_pre"], lp["w_ffn_up"], lp["ffn_conv_w"], lp["ffn_conv_b"],
                  seq=seq, tm=tiles["tm"], tn=tiles["tn"])
    return _ffn_down(act, x1, p, lp["w_ffn_down"], lp["norm_ffn_post"],
                     lp["w_ple_gate"], lp["w_ple_proj"], seq=seq, tm=tiles["tm_ffn"])


TILES = dict(tm=512, tn=1024, tq=512, tk=512, ts=512, tm_ffn=512)


def _trunk(x, p, layers, tiles):
    batch, seq, _ = x.shape
    rope = _rope_tables(seq)
    h = x.reshape(batch * seq, D_MODEL)
    for i, lp in enumerate(layers):
        lam_init = 0.8 - 0.6 * math.exp(-0.3 * i)
        h = _layer(h, p[i].reshape(batch * seq, PLE_DIM), lp, rope, lam_init,
                   batch=batch, seq=seq, tiles=tiles)
    return h.reshape(batch, seq, D_MODEL)


def kernel(x_prompt, x_sample, p_prompt, p_sample, norm_mix_pre, norm_mix_post, w_in, lam_q1, lam_k1, lam_q2, lam_k2, subln_g, lru_conv_w, lru_conv_b, rg_wa, rg_ba, rg_wx, rg_bx, rg_lambda, w_branch_attn, w_branch_lru, w_out, norm_ffn_pre, norm_ffn_post, w_ffn_up, ffn_conv_w, ffn_conv_b, w_ffn_down, w_ple_proj, w_ple_gate):
    w = dict(norm_mix_pre=norm_mix_pre, norm_mix_post=norm_mix_post, w_in=w_in, lam_q1=lam_q1,
             lam_k1=lam_k1, lam_q2=lam_q2, lam_k2=lam_k2, subln_g=subln_g, lru_conv_w=lru_conv_w,
             lru_conv_b=lru_conv_b, rg_wa=rg_wa, rg_ba=rg_ba, rg_wx=rg_wx, rg_bx=rg_bx,
             rg_lambda=rg_lambda, w_branch_attn=w_branch_attn, w_branch_lru=w_branch_lru, w_out=w_out,
             norm_ffn_pre=norm_ffn_pre, norm_ffn_post=norm_ffn_post, w_ffn_up=w_ffn_up,
             ffn_conv_w=ffn_conv_w, ffn_conv_b=ffn_conv_b, w_ffn_down=w_ffn_down,
             w_ple_proj=w_ple_proj, w_ple_gate=w_ple_gate)
    layers = [_prep_layer(i, w) for i in range(w_in.shape[0])]
    return (_trunk(x_prompt, p_prompt, layers, TILES), _trunk(x_sample, p_sample, layers, TILES))
```

```python
import functools
import math

import jax
import jax.numpy as jnp
from jax import lax
from jax.experimental import pallas as pl
from jax.experimental.pallas import tpu as pltpu

F32 = jnp.float32
BF16 = jnp.bfloat16

D_MODEL = 1024
N_HEADS = 8
HEAD_DIM = 64
V_DIM = 2 * HEAD_DIM
ATT_W = N_HEADS * V_DIM
LRU_W = 1024
N_LRU_BLOCKS = 8
LRU_BLOCK = LRU_W // N_LRU_BLOCKS
LRU_CONV_LEFT = 2
LRU_C = 8.0
D_FF = 3072
PLE_DIM = 256
EPS = 1e-6
TINY = 1e-30
ROPE_THETA = 10000.0
IN_COLS = 3 * ATT_W + 2 * LRU_W + 2 * D_MODEL
LOG2_E = math.log2(math.e)
KEY_CHUNKS_PER_ITER = 4
FFN_CONV_LEFT = 1
HALO = 16
GELU_K0 = math.sqrt(2.0 / math.pi)
GELU_K1 = 0.044715 * GELU_K0

LANES = 128
SUBLANES = 8
BF16_ROWS = 16
VMEM_LIMIT_BYTES = 56 * 1024 * 1024

COL_Q, COL_K, COL_V, COL_U, COL_GL, COL_GA, COL_GB = range(7)


def _cparams(*sem):
    return pltpu.CompilerParams(dimension_semantics=sem, vmem_limit_bytes=VMEM_LIMIT_BYTES)


def _rms(x, g):
    return x * lax.rsqrt(jnp.mean(x * x, axis=-1, keepdims=True) + EPS) * g


def _fill_normed(x_ref, xp_ref, xn_ref, g_ref, h_scr, *, tm, nseq):
    pos = pl.program_id(0) % nseq
    g = g_ref[...]
    h_scr[:HALO, :] = jnp.where(pos > 0, _rms(xp_ref[...], g), 0.0).astype(BF16)
    h_scr[HALO:HALO + tm, :] = _rms(x_ref[...], g).astype(BF16)
    h_scr[HALO + tm:, :] = jnp.where(pos < nseq - 1, _rms(xn_ref[...], g), 0.0).astype(BF16)


def _dwconv_rows(z_ext, cw, cb, *, left, tm):
    rows = z_ext.shape[0]
    out = cb
    for tap in range(cw.shape[0]):
        off = tap - left
        shifted = z_ext if off == 0 else pltpu.roll(z_ext, (-off) % rows, 0)
        out = out + shifted[HALO:HALO + tm] * cw[tap:tap + 1]
    return out


def _gelu2(x):
    return x * (1.0 + jnp.tanh(x * (GELU_K0 + GELU_K1 * (x * x))))


def _in_proj_kernel(x_ref, xp_ref, xn_ref, g_ref, w_ref, cos_ref, sa_ref, sb_ref, cw_ref, cb_ref,
                    o_ref, h_scr, *, tm, tn, nseq):
    _fill_normed(x_ref, xp_ref, xn_ref, g_ref, h_scr, tm=tm, nseq=nseq)
    for j in range(w_ref.shape[1] // tn):
        col0 = j * tn
        cols = slice(col0, col0 + tn)
        col_block = col0 // D_MODEL
        if col_block == COL_U:
            z_ext = jnp.dot(h_scr[...], w_ref[:, cols], preferred_element_type=F32)
            u0 = col0 - COL_U * D_MODEL
            xc = _dwconv_rows(z_ext, cw_ref[:, u0:u0 + tn], cb_ref[:, u0:u0 + tn],
                              left=LRU_CONV_LEFT, tm=tm)
            o_ref[:, cols] = xc.astype(o_ref.dtype)
            continue
        z = jnp.dot(h_scr[HALO:HALO + tm, :], w_ref[:, cols], preferred_element_type=F32)
        if col_block in (COL_Q, COL_K):
            scale = LOG2_E if col_block == COL_Q else 1.0
            cos, sa, sb = cos_ref[...] * scale, sa_ref[...] * scale, sb_ref[...] * scale
            for g in range(tn // LANES):
                zg = z[:, g * LANES:(g + 1) * LANES]
                r = (zg * cos + pltpu.roll(zg, LANES - HEAD_DIM // 2, 1) * sa
                     + pltpu.roll(zg, HEAD_DIM // 2, 1) * sb)
                o_ref[:, col0 + g * LANES:col0 + (g + 1) * LANES] = r.astype(o_ref.dtype)
        elif col_block == COL_GL:
            o_ref[:, cols] = (0.5 * _gelu2(z)).astype(o_ref.dtype)
        else:
            o_ref[:, cols] = z.astype(o_ref.dtype)


def _ffn_up_kernel(x_ref, xp_ref, xn_ref, g_ref, w_ref, cw_ref, cb_ref, o_ref, h_scr, *, tm, tn, nseq):
    _fill_normed(x_ref, xp_ref, xn_ref, g_ref, h_scr, tm=tm, nseq=nseq)
    for j in range(D_FF // tn):
        cols = slice(j * tn, (j + 1) * tn)
        c_ext = jnp.dot(h_scr[...], w_ref[:, cols], preferred_element_type=F32)
        lin = jnp.dot(h_scr[HALO:HALO + tm, :], w_ref[:, D_FF + j * tn:D_FF + (j + 1) * tn],
                      preferred_element_type=F32)
        conv = _dwconv_rows(c_ext, cw_ref[:, cols], cb_ref[:, cols], left=FFN_CONV_LEFT, tm=tm)
        o_ref[:, cols] = (_gelu2(conv) * lin).astype(o_ref.dtype)


def _norm_matmul(kernel_fn, x, g, w, extra, n_out, *, seq, tm, tn, name):
    T, K = x.shape
    tm = min(tm, seq)
    nseq = seq // tm
    assert w.shape[1] % tn == 0 and D_MODEL % tn == 0 and tm % HALO == 0
    halo_per_tile = tm // HALO
    last_halo = T // HALO - 1
    const = lambda i: (0, 0)
    in_specs = [
        pl.BlockSpec((tm, K), lambda i: (i, 0)),
        pl.BlockSpec((HALO, K), lambda i: (jnp.maximum(i * halo_per_tile - 1, 0), 0)),
        pl.BlockSpec((HALO, K), lambda i: (jnp.minimum((i + 1) * halo_per_tile, last_halo), 0)),
        pl.BlockSpec((1, K), const),
        pl.BlockSpec(w.shape, const, pipeline_mode=pl.Buffered(1)),
    ] + [spec for _, spec in extra]
    return pl.pallas_call(
        functools.partial(kernel_fn, tm=tm, tn=tn, nseq=nseq),
        grid=(T // tm,),
        in_specs=in_specs,
        out_specs=pl.BlockSpec((tm, n_out), lambda i: (i, 0)),
        out_shape=jax.ShapeDtypeStruct((T, n_out), BF16),
        scratch_shapes=[pltpu.VMEM((tm + 2 * HALO, K), BF16)],
        compiler_params=_cparams("parallel"),
        name=name,
    )(x, x, x, g, w, *[a for a, _ in extra])


def _in_proj(x, g, w, rope, conv_w, conv_b, *, seq, tm, tn):
    tm = min(tm, seq)
    nseq = seq // tm
    const = lambda i: (0, 0)
    tab_spec = pl.BlockSpec((tm, LANES), lambda i: (i % nseq, 0))
    extra = [(t, tab_spec) for t in rope]
    extra += [(conv_w, pl.BlockSpec(conv_w.shape, const)), (conv_b, pl.BlockSpec(conv_b.shape, const))]
    return _norm_matmul(_in_proj_kernel, x, g, w, extra, IN_COLS, seq=seq, tm=tm, tn=tn, name="in_proj")


def _ffn_up(x, g, w, conv_w, conv_b, *, seq, tm, tn):
    const = lambda i: (0, 0)
    extra = [(conv_w, pl.BlockSpec(conv_w.shape, const)), (conv_b, pl.BlockSpec(conv_b.shape, const))]
    return _norm_matmul(_ffn_up_kernel, x, g, w, extra, D_FF, seq=seq, tm=tm, tn=tn, name="ffn_up")


def _attn_kernel(lamp_ref, g_ref, q_ref, k_ref, v_ref, o_ref, qt_scr, vt_scr, acc_scr, st_scr, mx_scr,
                 *, lam_init, seq, tq, tk, group):
    nq = seq // tq
    n_chunks = seq // tk
    n_groups = n_chunks // group

    vt_scr[V_DIM:, :] = jnp.ones((BF16_ROWS, seq), BF16)
    for c in range(n_chunks):
        sl = slice(c * tk, (c + 1) * tk)
        vt_scr[:V_DIM, sl] = v_ref[sl, :].astype(F32).T.astype(BF16)

    lp = lamp_ref[...]
    lam = (jnp.exp(jnp.sum(lp[0:1] * lp[1:2], axis=-1, keepdims=True))
           - jnp.exp(jnp.sum(lp[2:3] * lp[3:4], axis=-1, keepdims=True)) + lam_init)
    first_map = lax.broadcasted_iota(jnp.int32, (V_DIM, tq), 0) < HEAD_DIM

    def prep_q(qi, qslot):
        r0 = pl.multiple_of(jnp.minimum(qi, nq - 1) * tq, tq)
        qt = q_ref[pl.ds(r0, tq), :].astype(F32).T
        qt_scr[qslot, 0] = jnp.where(first_map, qt, 0.0).astype(BF16)
        qt_scr[qslot, 1] = jnp.where(first_map, 0.0, qt).astype(BF16)

    def scores(c, slot, qslot):
        r0 = pl.multiple_of(c * tk, tk)
        k_c = k_ref[pl.ds(r0, tk), :]
        for mp in range(2):
            st = jnp.dot(k_c, qt_scr[qslot, mp], preferred_element_type=F32)
            st_scr[slot, mp] = st
            mx_scr[slot, mp] = jnp.max(st, axis=0, keepdims=True)

    def softmax_pv(c, slot, ms):
        r0 = pl.multiple_of(c * tk, tk)
        vt_c = vt_scr[:, pl.ds(r0, tk)]
        new_m = []
        for mp in range(2):
            m_new = jnp.maximum(ms[mp], mx_scr[slot, mp])
            alpha = jnp.exp2(ms[mp] - m_new)
            et = jnp.exp2(st_scr[slot, mp] - m_new).astype(BF16)
            acc_scr[mp] = alpha * acc_scr[mp] + jnp.dot(vt_c, et, preferred_element_type=F32)
            new_m.append(m_new)
        return tuple(new_m)

    def chunk_group(c0, ms, qslot, last):
        for i in range(group):
            if last and i == group - 1:
                scores(0, 0, 1 - qslot)
            else:
                scores(c0 + i + 1, (i + 1) % 2, qslot)
            ms = softmax_pv(c0 + i, i % 2, ms)
        return ms

    def query_tile(qi, carry):
        qslot = qi % 2
        prep_q(qi + 1, 1 - qslot)
        acc_scr[...] = jnp.zeros(acc_scr.shape, F32)
        neg_inf = jnp.full((1, tq), -jnp.inf, F32)
        ms = lax.fori_loop(
            0, n_groups - 1,
            lambda p, ms: chunk_group(group * p, ms, qslot, False),
            (neg_inf, neg_inf))
        chunk_group(group * (n_groups - 1), ms, qslot, True)
        acc0, acc1 = acc_scr[0], acc_scr[1]
        ot = (acc0[:V_DIM] * (1.0 / acc0[V_DIM:V_DIM + 1])
              - acc1[:V_DIM] * (lam / acc1[V_DIM:V_DIM + 1]))
        r0 = pl.multiple_of(qi * tq, tq)
        o_ref[pl.ds(r0, tq), :] = (_rms(ot.T, g_ref[...]) * (1.0 - lam_init)).astype(o_ref.dtype)
        return carry

    prep_q(0, 0)
    scores(0, 0, 0)
    lax.fori_loop(0, nq, query_tile, 0)


def _attention(z, lamp, subln_g, *, batch, seq, lam_init, tq, tk):
    T = z.shape[0]
    tq, tk = min(tq, seq), min(tk, seq // 2)
    group = min(KEY_CHUNKS_PER_ITER, seq // tk)
    assert group % 2 == 0 and seq % (group * tk) == 0 and seq % tq == 0
    seq_block = lambda col: pl.BlockSpec((seq, V_DIM), lambda b, h: (b, col * N_HEADS + h))
    return pl.pallas_call(
        functools.partial(_attn_kernel, lam_init=lam_init, seq=seq, tq=tq, tk=tk, group=group),
        grid=(batch, N_HEADS),
        in_specs=[
            pl.BlockSpec((4, HEAD_DIM), lambda b, h: (0, 0)),
            pl.BlockSpec((1, V_DIM), lambda b, h: (0, 0)),
            seq_block(COL_Q), seq_block(COL_K), seq_block(COL_V),
        ],
        out_specs=pl.BlockSpec((seq, V_DIM), lambda b, h: (b, h)),
        out_shape=jax.ShapeDtypeStruct((T, ATT_W), BF16),
        scratch_shapes=[pltpu.VMEM((2, 2, V_DIM, tq), BF16),
                        pltpu.VMEM((V_DIM + BF16_ROWS, seq), BF16),
                        pltpu.VMEM((2, V_DIM + BF16_ROWS, tq), F32),
                        pltpu.VMEM((2, 2, tk, tq), F32),
                        pltpu.VMEM((2, 2, 1, tq), F32)],
        compiler_params=_cparams("parallel", "parallel"),
        name="diff_attention",
    )(lamp, subln_g, z, z, z)


def _softplus(y):
    return jnp.maximum(y, 0.0) + jnp.log1p(jnp.exp(-jnp.abs(y)))


def _lru_gates(xc_ref, wri_ref, bri_ref, lam_ref, a_scr, b_scr):
    xcb = xc_ref[...]
    xc = xcb.astype(F32)
    c_softplus = LRU_C * _softplus(-lam_ref[...])
    for n in range(N_LRU_BLOCKS):
        sl = slice(n * LRU_BLOCK, (n + 1) * LRU_BLOCK)
        t = jnp.tanh(jnp.dot(xcb[:, sl], wri_ref[n], preferred_element_type=F32) + bri_ref[n])
        r = 0.5 * t[:, :LRU_BLOCK] + 0.5
        i = 0.5 * t[:, LRU_BLOCK:] + 0.5
        neg_log_a = r * c_softplus[:, sl]
        a = jnp.exp2(neg_log_a * (-LOG2_E))
        a_scr[:, sl] = a
        one_minus_a2 = jnp.tanh(neg_log_a) * (a * a + 1.0)
        root = one_minus_a2 * lax.rsqrt(jnp.maximum(one_minus_a2, TINY))
        b_scr[:, sl] = root * (i * xc[:, sl])


def _lru_scan(a_scr, b_scr, h_scr, *, reverse, ts, unrolled):
    rows = lax.broadcasted_iota(jnp.int32, (SUBLANES, LRU_W), 0)
    ng = ts // SUBLANES

    def group(g, h):
        gi = (ng - 1 - g) if reverse else g
        r0 = gi * SUBLANES if unrolled else pl.multiple_of(gi * SUBLANES, SUBLANES)
        a8 = a_scr[pl.ds(r0, SUBLANES), :]
        b8 = b_scr[pl.ds(r0, SUBLANES), :]
        for d in (1, 2, 4):
            shift = (SUBLANES - d) if reverse else d
            keep = (rows < SUBLANES - d) if reverse else (rows >= d)
            a_sh = jnp.where(keep, pltpu.roll(a8, shift, 0), 1.0)
            b_sh = jnp.where(keep, pltpu.roll(b8, shift, 0), 0.0)
            b8 = a8 * b_sh + b8
            a8 = a8 * a_sh
        h8 = a8 * h + b8
        b_scr[pl.ds(r0, SUBLANES), :] = h8
        return h8[0:1, :] if reverse else h8[SUBLANES - 1:SUBLANES, :]

    if unrolled:
        h = h_scr[...]
        for g in range(ng):
            h = group(g, h)
        h_scr[...] = h
    else:
        h_scr[...] = lax.fori_loop(0, ng, group, h_scr[...])


def _lru_fwd_kernel(xc_ref, wri_ref, bri_ref, lam_ref, o_ref, a_scr, b_scr, h_scr, *, ts):
    @pl.when(pl.program_id(1) == 0)
    def _():
        h_scr[...] = jnp.zeros(h_scr.shape, F32)

    _lru_gates(xc_ref, wri_ref, bri_ref, lam_ref, a_scr, b_scr)
    _lru_scan(a_scr, b_scr, h_scr, reverse=False, ts=ts, unrolled=False)
    o_ref[...] = b_scr[...]


def _lru_bwd_merge_kernel(xc_ref, wri_ref, bri_ref, lam_ref, gl_ref, hf_ref, oa_ref, ga_ref, gb_ref,
                          x_ref, wba_ref, wbl_ref, wo_ref, g_ref, o_ref, a_scr, b_scr, h_scr, *, ts):
    @pl.when(pl.program_id(1) == 0)
    def _():
        h_scr[...] = jnp.zeros(h_scr.shape, F32)

    _lru_gates(xc_ref, wri_ref, bri_ref, lam_ref, a_scr, b_scr)
    _lru_scan(a_scr, b_scr, h_scr, reverse=True, ts=ts, unrolled=True)
    o_lru = (gl_ref[...].astype(F32) * (hf_ref[...] + b_scr[...])).astype(BF16)
    pa = jnp.dot(oa_ref[...], wba_ref[...], preferred_element_type=F32)
    pb = jnp.dot(o_lru, wbl_ref[...], preferred_element_type=F32)
    m = ((jnp.tanh(ga_ref[...].astype(F32)) + 1.0) * pa
         + (jnp.tanh(gb_ref[...].astype(F32)) + 1.0) * pb)
    y = jnp.dot(m.astype(BF16), wo_ref[...], preferred_element_type=F32)
    o_ref[...] = x_ref[...] + _rms(y, g_ref[...])


def _lru_fwd(z, wri, bri, lam, *, batch, seq, ts):
    T = z.shape[0]
    ts = min(ts, seq)
    ns = seq // ts
    return pl.pallas_call(
        functools.partial(_lru_fwd_kernel, ts=ts),
        grid=(batch, ns),
        in_specs=[
            pl.BlockSpec((ts, LRU_W), lambda b, j: (b * ns + j, COL_U)),
            pl.BlockSpec(wri.shape, lambda b, j: (0, 0, 0)),
            pl.BlockSpec(bri.shape, lambda b, j: (0, 0, 0)),
            pl.BlockSpec((1, LRU_W), lambda b, j: (0, 0)),
        ],
        out_specs=pl.BlockSpec((ts, LRU_W), lambda b, j: (b * ns + j, 0)),
        out_shape=jax.ShapeDtypeStruct((T, LRU_W), F32),
        scratch_shapes=[pltpu.VMEM((ts, LRU_W), F32), pltpu.VMEM((ts, LRU_W), F32),
                        pltpu.VMEM((1, LRU_W), F32)],
        compiler_params=_cparams("parallel", "arbitrary"),
        name="lru_fwd",
    )(z, wri, bri, lam)


def _lru_bwd_merge(z, wri, bri, lam, h_fwd, o_attn, x, w_ba, w_bl, w_out, g, *, batch, seq, ts):
    T = z.shape[0]
    ts = min(ts, seq)
    ns = seq // ts
    tile = lambda col: pl.BlockSpec((ts, D_MODEL), lambda b, j: (b * ns + ns - 1 - j, col))
    const2 = lambda b, j: (0, 0)
    const3 = lambda b, j: (0, 0, 0)
    wspec = pl.BlockSpec((D_MODEL, D_MODEL), const2, pipeline_mode=pl.Buffered(1))
    return pl.pallas_call(
        functools.partial(_lru_bwd_merge_kernel, ts=ts),
        grid=(batch, ns),
        in_specs=[
            tile(COL_U),
            pl.BlockSpec(wri.shape, const3),
            pl.BlockSpec(bri.shape, const3),
            pl.BlockSpec((1, LRU_W), const2),
            tile(COL_GL), tile(0), tile(0), tile(COL_GA), tile(COL_GB), tile(0),
            wspec, wspec, wspec,
            pl.BlockSpec((1, D_MODEL), const2),
        ],
        out_specs=tile(0),
        out_shape=jax.ShapeDtypeStruct((T, D_MODEL), F32),
        scratch_shapes=[pltpu.VMEM((ts, LRU_W), F32), pltpu.VMEM((ts, LRU_W), F32),
                        pltpu.VMEM((1, LRU_W), F32)],
        compiler_params=_cparams("parallel", "arbitrary"),
        name="lru_bwd_merge",
    )(z, wri, bri, lam, z, h_fwd, o_attn, z, z, x, w_ba, w_bl, w_out, g)


def _ffn_down_kernel(act_ref, x_ref, p_ref, wd_ref, g_ref, wg_ref, wp_ref, o_ref):
    f = jnp.dot(act_ref[...], wd_ref[...], preferred_element_type=F32)
    x = x_ref[...] + _rms(f, g_ref[...])
    gate2 = jnp.tanh(jnp.dot(x.astype(BF16), wg_ref[...], preferred_element_type=F32)) + 1.0
    emb_half = jnp.dot(p_ref[...].astype(BF16), wp_ref[...], preferred_element_type=F32)
    o_ref[...] = x + gate2 * emb_half


def _ffn_down(act, x, p, w_down, g, w_gate, w_proj, *, seq, tm):
    T = x.shape[0]
    tm = min(tm, seq)
    row = lambda i: (i, 0)
    const = lambda i: (0, 0)
    return pl.pallas_call(
        _ffn_down_kernel,
        grid=(T // tm,),
        in_specs=[
            pl.BlockSpec((tm, D_FF), row),
            pl.BlockSpec((tm, D_MODEL), row),
            pl.BlockSpec((tm, PLE_DIM), row),
            pl.BlockSpec((D_FF, D_MODEL), const, pipeline_mode=pl.Buffered(1)),
            pl.BlockSpec((1, D_MODEL), const),
            pl.BlockSpec((D_MODEL, D_MODEL), const, pipeline_mode=pl.Buffered(1)),
            pl.BlockSpec((PLE_DIM, D_MODEL), const, pipeline_mode=pl.Buffered(1)),
        ],
        out_specs=pl.BlockSpec((tm, D_MODEL), row),
        out_shape=jax.ShapeDtypeStruct((T, D_MODEL), F32),
        compiler_params=_cparams("parallel"),
        name="ffn_down_ple",
    )(act, x, p, w_down, g, w_gate, w_proj)


def _rope_tables(seq):
    half = HEAD_DIM // 2
    inv = ROPE_THETA ** (-jnp.arange(0, HEAD_DIM, 2, dtype=F32) / HEAD_DIM)
    ang = jnp.arange(seq, dtype=F32)[:, None] * inv[None, :]
    lane = jnp.arange(LANES)
    cos = jnp.cos(ang)[:, lane % half]
    sin = jnp.sin(ang)[:, lane % half]
    lower = (lane % HEAD_DIM) < half
    return cos, jnp.where(lower, -sin, 0.0), jnp.where(lower, 0.0, sin)


def _prep_layer(i, w):
    col = jnp.arange(IN_COLS)
    q_scale = jnp.where(col < ATT_W, HEAD_DIM ** -0.5,
                        jnp.where(col >= COL_GA * D_MODEL, 0.5, 1.0)).astype(F32)
    row = lambda a: a.reshape(1, -1).astype(F32)
    wri = (0.5 * jnp.concatenate([w["rg_wa"][i], w["rg_wx"][i]], axis=-1)).astype(BF16)
    bri = 0.5 * jnp.concatenate([w["rg_ba"][i].reshape(2, N_LRU_BLOCKS, 1, LRU_BLOCK),
                                 w["rg_bx"][i].reshape(2, N_LRU_BLOCKS, 1, LRU_BLOCK)], axis=-1)
    return dict(
        norm_mix_pre=row(w["norm_mix_pre"][i]),
        norm_mix_post=row(w["norm_mix_post"][i]),
        w_in=(w["w_in"][i] * q_scale).astype(BF16),
        lamp=jnp.stack([w["lam_q1"][i], w["lam_k1"][i], w["lam_q2"][i], w["lam_k2"][i]]).astype(F32),
        subln_g=row(w["subln_g"][i]),
        lru_conv_w=w["lru_conv_w"][i].astype(F32),
        lru_conv_b=row(w["lru_conv_b"][i]),
        wri=wri, bri=bri.astype(F32),
        rg_lambda=w["rg_lambda"][i].reshape(2, 1, LRU_W).astype(F32),
        w_branch_attn=w["w_branch_attn"][i].astype(BF16),
        w_branch_lru=w["w_branch_lru"][i].astype(BF16),
        w_out=(0.5 * w["w_out"][i]).astype(BF16),
        norm_ffn_pre=row(w["norm_ffn_pre"][i]),
        norm_ffn_post=row(w["norm_ffn_post"][i]),
        w_ffn_up=w["w_ffn_up"][i].astype(BF16),
        ffn_conv_w=w["ffn_conv_w"][i].astype(F32),
        ffn_conv_b=row(w["ffn_conv_b"][i]),
        w_ffn_down=(0.5 * w["w_ffn_down"][i]).astype(BF16),
        w_ple_proj=(0.5 * w["w_ple_proj"][i]).astype(BF16),
        w_ple_gate=(0.5 * w["w_ple_gate"][i]).astype(BF16),
    )


def _layer(x, p, lp, rope, lam_init, *, batch, seq, tiles):
    z = _in_proj(x, lp["norm_mix_pre"], lp["w_in"], rope, lp["lru_conv_w"], lp["lru_conv_b"],
                 seq=seq, tm=tiles["tm"], tn=tiles["tn"])
    o_attn = _attention(z, lp["lamp"], lp["subln_g"], batch=batch, seq=seq, lam_init=lam_init,
                        tq=tiles["tq"], tk=tiles["tk"])
    h_fwd = _lru_fwd(z, lp["wri"][0], lp["bri"][0], lp["rg_lambda"][0],
                     batch=batch, seq=seq, ts=tiles["ts"])
    x1 = _lru_bwd_merge(z, lp["wri"][1], lp["bri"][1], lp["rg_lambda"][1], h_fwd, o_attn, x,
                        lp["w_branch_attn"], lp["w_branch_lru"], lp["w_out"], lp["norm_mix_post"],
                        batch=batch, seq=seq, ts=tiles["ts_merge"])
    act = _ffn_up(x1, lp["norm_ffn_pre"], lp["w_ffn_up"], lp["ffn_conv_w"], lp["ffn_conv_b"],
                  seq=seq, tm=tiles["tm"], tn=tiles["tn"])
    return _ffn_down(act, x1, p, lp["w_ffn_down"], lp["norm_ffn_post"],
                     lp["w_ple_gate"], lp["w_ple_proj"], seq=seq, tm=tiles["tm_ffn"])


TILES = dict(tm=512, tn=1024, tq=512, tk=512, ts=1024, ts_merge=512, tm_ffn=512)


def _trunk(x, p, layers, tiles):
    batch, seq, _ = x.shape
    rope = _rope_tables(seq)
    h = x.reshape(batch * seq, D_MODEL)
    for i, lp in enumerate(layers):
        lam_init = 0.8 - 0.6 * math.exp(-0.3 * i)
        h = _layer(h, p[i].reshape(batch * seq, PLE_DIM), lp, rope, lam_init,
                   batch=batch, seq=seq, tiles=tiles)
    return h.reshape(batch, seq, D_MODEL)


def kernel(x_prompt, x_sample, p_prompt, p_sample, norm_mix_pre, norm_mix_post, w_in, lam_q1, lam_k1, lam_q2, lam_k2, subln_g, lru_conv_w, lru_conv_b, rg_wa, rg_ba, rg_wx, rg_bx, rg_lambda, w_branch_attn, w_branch_lru, w_out, norm_ffn_pre, norm_ffn_post, w_ffn_up, ffn_conv_w, ffn_conv_b, w_ffn_down, w_ple_proj, w_ple_gate):
    w = dict(norm_mix_pre=norm_mix_pre, norm_mix_post=norm_mix_post, w_in=w_in, lam_q1=lam_q1,
             lam_k1=lam_k1, lam_q2=lam_q2, lam_k2=lam_k2, subln_g=subln_g, lru_conv_w=lru_conv_w,
             lru_conv_b=lru_conv_b, rg_wa=rg_wa, rg_ba=rg_ba, rg_wx=rg_wx, rg_bx=rg_bx,
             rg_lambda=rg_lambda, w_branch_attn=w_branch_attn, w_branch_lru=w_branch_lru, w_out=w_out,
             norm_ffn_pre=norm_ffn_pre, norm_ffn_post=norm_ffn_post, w_ffn_up=w_ffn_up,
             ffn_conv_w=ffn_conv_w, ffn_conv_b=ffn_conv_b, w_ffn_down=w_ffn_down,
             w_ple_proj=w_ple_proj, w_ple_gate=w_ple_gate)
    layers = [_prep_layer(i, w) for i in range(w_in.shape[0])]
    return (_trunk(x_prompt, p_prompt, layers, TILES), _trunk(x_sample, p_sample, layers, TILES))
```

```python
import functools
import math

import jax
import jax.numpy as jnp
from jax import lax
from jax.experimental import pallas as pl
from jax.experimental.pallas import tpu as pltpu

F32 = jnp.float32
BF16 = jnp.bfloat16

D_MODEL = 1024
N_HEADS = 8
HEAD_DIM = 64
V_DIM = 2 * HEAD_DIM
ATT_W = N_HEADS * V_DIM
LRU_W = 1024
N_LRU_BLOCKS = 8
LRU_BLOCK = LRU_W // N_LRU_BLOCKS
LRU_CONV_LEFT = 2
LRU_C = 8.0
D_FF = 3072
PLE_DIM = 256
EPS = 1e-6
TINY = 1e-30
ROPE_THETA = 10000.0
IN_COLS = 3 * ATT_W + 2 * LRU_W + 2 * D_MODEL
LOG2_E = math.log2(math.e)
KEY_CHUNKS_PER_ITER = 4
FFN_CONV_LEFT = 1
HALO = 16
GELU_K0 = math.sqrt(2.0 / math.pi)
GELU_K1 = 0.044715 * GELU_K0

LANES = 128
SUBLANES = 8
BF16_ROWS = 16
VMEM_LIMIT_BYTES = 56 * 1024 * 1024

COL_Q, COL_K, COL_V, COL_U, COL_GL, COL_GA, COL_GB = range(7)


def _cparams(*sem):
    return pltpu.CompilerParams(dimension_semantics=sem, vmem_limit_bytes=VMEM_LIMIT_BYTES)


def _rms(x, g):
    return x * lax.rsqrt(jnp.mean(x * x, axis=-1, keepdims=True) + EPS) * g


def _fill_normed(x_ref, xp_ref, xn_ref, g_ref, h_scr, *, tm, nseq):
    pos = pl.program_id(0) % nseq
    g = g_ref[...]
    h_scr[:HALO, :] = jnp.where(pos > 0, _rms(xp_ref[...], g), 0.0).astype(BF16)
    h_scr[HALO:HALO + tm, :] = _rms(x_ref[...], g).astype(BF16)
    h_scr[HALO + tm:, :] = jnp.where(pos < nseq - 1, _rms(xn_ref[...], g), 0.0).astype(BF16)


def _dwconv_rows(z_ext, cw, cb, *, left, tm):
    rows = z_ext.shape[0]
    out = cb
    for tap in range(cw.shape[0]):
        off = tap - left
        shifted = z_ext if off == 0 else pltpu.roll(z_ext, (-off) % rows, 0)
        out = out + shifted[HALO:HALO + tm] * cw[tap:tap + 1]
    return out


def _gelu2(x):
    return x * (1.0 + jnp.tanh(x * (GELU_K0 + GELU_K1 * (x * x))))


def _in_proj_kernel(x_ref, xp_ref, xn_ref, g_ref, w_ref, cos_ref, sa_ref, sb_ref, cw_ref, cb_ref,
                    o_ref, h_scr, *, tm, tn, nseq):
    _fill_normed(x_ref, xp_ref, xn_ref, g_ref, h_scr, tm=tm, nseq=nseq)
    for j in range(w_ref.shape[1] // tn):
        col0 = j * tn
        cols = slice(col0, col0 + tn)
        col_block = col0 // D_MODEL
        if col_block == COL_U:
            z_ext = jnp.dot(h_scr[...], w_ref[:, cols], preferred_element_type=F32)
            u0 = col0 - COL_U * D_MODEL
            xc = _dwconv_rows(z_ext, cw_ref[:, u0:u0 + tn], cb_ref[:, u0:u0 + tn],
                              left=LRU_CONV_LEFT, tm=tm)
            o_ref[:, cols] = xc.astype(o_ref.dtype)
            continue
        z = jnp.dot(h_scr[HALO:HALO + tm, :], w_ref[:, cols], preferred_element_type=F32)
        if col_block in (COL_Q, COL_K):
            scale = LOG2_E if col_block == COL_Q else 1.0
            cos, sa, sb = cos_ref[...] * scale, sa_ref[...] * scale, sb_ref[...] * scale
            for g in range(tn // LANES):
                zg = z[:, g * LANES:(g + 1) * LANES]
                r = (zg * cos + pltpu.roll(zg, LANES - HEAD_DIM // 2, 1) * sa
                     + pltpu.roll(zg, HEAD_DIM // 2, 1) * sb)
                o_ref[:, col0 + g * LANES:col0 + (g + 1) * LANES] = r.astype(o_ref.dtype)
        elif col_block == COL_GL:
            o_ref[:, cols] = (0.5 * _gelu2(z)).astype(o_ref.dtype)
        else:
            o_ref[:, cols] = z.astype(o_ref.dtype)


def _ffn_kernel(x_ref, xp_ref, xn_ref, g_ref, w_ref, cw_ref, cb_ref, p_ref, wd_ref, gpost_ref,
                wg_ref, wp_ref, o_ref, h_scr, *, tm, tn, nseq):
    _fill_normed(x_ref, xp_ref, xn_ref, g_ref, h_scr, tm=tm, nseq=nseq)
    f = None
    for j in range(D_FF // tn):
        cols = slice(j * tn, (j + 1) * tn)
        c_ext = jnp.dot(h_scr[...], w_ref[:, cols], preferred_element_type=F32)
        lin = jnp.dot(h_scr[HALO:HALO + tm, :], w_ref[:, D_FF + j * tn:D_FF + (j + 1) * tn],
                      preferred_element_type=F32)
        conv = _dwconv_rows(c_ext, cw_ref[:, cols], cb_ref[:, cols], left=FFN_CONV_LEFT, tm=tm)
        act = (_gelu2(conv) * lin).astype(BF16)
        part = jnp.dot(act, wd_ref[cols, :], preferred_element_type=F32)
        f = part if f is None else f + part
    x = x_ref[...] + _rms(f, gpost_ref[...])
    gate2 = jnp.tanh(jnp.dot(x.astype(BF16), wg_ref[...], preferred_element_type=F32)) + 1.0
    emb_half = jnp.dot(p_ref[...].astype(BF16), wp_ref[...], preferred_element_type=F32)
    o_ref[...] = x + gate2 * emb_half


def _ffn(x, p, g_pre, w_up, conv_w, conv_b, w_down, g_post, w_gate, w_proj, *, seq, tm, tn):
    T, K = x.shape
    tm = min(tm, seq)
    nseq = seq // tm
    assert D_FF % tn == 0 and tm % HALO == 0
    halo_per_tile = tm // HALO
    last_halo = T // HALO - 1
    row = lambda i: (i, 0)
    const = lambda i: (0, 0)
    resident = lambda a: pl.BlockSpec(a.shape, const, pipeline_mode=pl.Buffered(1))
    return pl.pallas_call(
        functools.partial(_ffn_kernel, tm=tm, tn=tn, nseq=nseq),
        grid=(T // tm,),
        in_specs=[
            pl.BlockSpec((tm, K), row),
            pl.BlockSpec((HALO, K), lambda i: (jnp.maximum(i * halo_per_tile - 1, 0), 0)),
            pl.BlockSpec((HALO, K), lambda i: (jnp.minimum((i + 1) * halo_per_tile, last_halo), 0)),
            pl.BlockSpec((1, K), const),
            resident(w_up),
            pl.BlockSpec(conv_w.shape, const),
            pl.BlockSpec(conv_b.shape, const),
            pl.BlockSpec((tm, PLE_DIM), row),
            resident(w_down),
            pl.BlockSpec((1, D_MODEL), const),
            resident(w_gate), resident(w_proj),
        ],
        out_specs=pl.BlockSpec((tm, D_MODEL), row),
        out_shape=jax.ShapeDtypeStruct((T, D_MODEL), F32),
        scratch_shapes=[pltpu.VMEM((tm + 2 * HALO, K), BF16)],
        compiler_params=_cparams("parallel"),
        name="ffn_ple",
    )(x, x, x, g_pre, w_up, conv_w, conv_b, p, w_down, g_post, w_gate, w_proj)


def _norm_matmul(kernel_fn, x, g, w, extra, n_out, *, seq, tm, tn, name):
    T, K = x.shape
    tm = min(tm, seq)
    nseq = seq // tm
    assert w.shape[1] % tn == 0 and D_MODEL % tn == 0 and tm % HALO == 0
    halo_per_tile = tm // HALO
    last_halo = T // HALO - 1
    const = lambda i: (0, 0)
    in_specs = [
        pl.BlockSpec((tm, K), lambda i: (i, 0)),
        pl.BlockSpec((HALO, K), lambda i: (jnp.maximum(i * halo_per_tile - 1, 0), 0)),
        pl.BlockSpec((HALO, K), lambda i: (jnp.minimum((i + 1) * halo_per_tile, last_halo), 0)),
        pl.BlockSpec((1, K), const),
        pl.BlockSpec(w.shape, const, pipeline_mode=pl.Buffered(1)),
    ] + [spec for _, spec in extra]
    return pl.pallas_call(
        functools.partial(kernel_fn, tm=tm, tn=tn, nseq=nseq),
        grid=(T // tm,),
        in_specs=in_specs,
        out_specs=pl.BlockSpec((tm, n_out), lambda i: (i, 0)),
        out_shape=jax.ShapeDtypeStruct((T, n_out), BF16),
        scratch_shapes=[pltpu.VMEM((tm + 2 * HALO, K), BF16)],
        compiler_params=_cparams("parallel"),
        name=name,
    )(x, x, x, g, w, *[a for a, _ in extra])


def _in_proj(x, g, w, rope, conv_w, conv_b, *, seq, tm, tn):
    tm = min(tm, seq)
    nseq = seq // tm
    const = lambda i: (0, 0)
    tab_spec = pl.BlockSpec((tm, LANES), lambda i: (i % nseq, 0))
    extra = [(t, tab_spec) for t in rope]
    extra += [(conv_w, pl.BlockSpec(conv_w.shape, const)), (conv_b, pl.BlockSpec(conv_b.shape, const))]
    return _norm_matmul(_in_proj_kernel, x, g, w, extra, IN_COLS, seq=seq, tm=tm, tn=tn, name="in_proj")


def _attn_kernel(lamp_ref, g_ref, q_ref, k_ref, v_ref, o_ref, qt_scr, vt_scr, acc_scr, st_scr, mx_scr,
                 *, lam_init, seq, tq, tk, group):
    nq = seq // tq
    n_chunks = seq // tk
    n_groups = n_chunks // group

    vt_scr[V_DIM:, :] = jnp.ones((BF16_ROWS, seq), BF16)
    for c in range(n_chunks):
        sl = slice(c * tk, (c + 1) * tk)
        vt_scr[:V_DIM, sl] = v_ref[sl, :].astype(F32).T.astype(BF16)

    lp = lamp_ref[...]
    lam = (jnp.exp(jnp.sum(lp[0:1] * lp[1:2], axis=-1, keepdims=True))
           - jnp.exp(jnp.sum(lp[2:3] * lp[3:4], axis=-1, keepdims=True)) + lam_init)
    first_map = lax.broadcasted_iota(jnp.int32, (V_DIM, tq), 0) < HEAD_DIM

    def prep_q(qi, qslot):
        r0 = pl.multiple_of(jnp.minimum(qi, nq - 1) * tq, tq)
        qt = q_ref[pl.ds(r0, tq), :].astype(F32).T
        qt_scr[qslot, 0] = jnp.where(first_map, qt, 0.0).astype(BF16)
        qt_scr[qslot, 1] = jnp.where(first_map, 0.0, qt).astype(BF16)

    def scores(c, slot, qslot):
        r0 = pl.multiple_of(c * tk, tk)
        k_c = k_ref[pl.ds(r0, tk), :]
        for mp in range(2):
            st = jnp.dot(k_c, qt_scr[qslot, mp], preferred_element_type=F32)
            st_scr[slot, mp] = st
            mx_scr[slot, mp] = jnp.max(st, axis=0, keepdims=True)

    def softmax_pv(c, slot, ms):
        r0 = pl.multiple_of(c * tk, tk)
        vt_c = vt_scr[:, pl.ds(r0, tk)]
        new_m = []
        for mp in range(2):
            m_new = jnp.maximum(ms[mp], mx_scr[slot, mp])
            alpha = jnp.exp2(ms[mp] - m_new)
            et = jnp.exp2(st_scr[slot, mp] - m_new).astype(BF16)
            acc_scr[mp] = alpha * acc_scr[mp] + jnp.dot(vt_c, et, preferred_element_type=F32)
            new_m.append(m_new)
        return tuple(new_m)

    def chunk_group(c0, ms, qslot, last):
        for i in range(group):
            if last and i == group - 1:
                scores(0, 0, 1 - qslot)
            else:
                scores(c0 + i + 1, (i + 1) % 2, qslot)
            ms = softmax_pv(c0 + i, i % 2, ms)
        return ms

    def query_tile(qi, carry):
        qslot = qi % 2
        prep_q(qi + 1, 1 - qslot)
        acc_scr[...] = jnp.zeros(acc_scr.shape, F32)
        neg_inf = jnp.full((1, tq), -jnp.inf, F32)
        ms = lax.fori_loop(
            0, n_groups - 1,
            lambda p, ms: chunk_group(group * p, ms, qslot, False),
            (neg_inf, neg_inf))
        chunk_group(group * (n_groups - 1), ms, qslot, True)
        acc0, acc1 = acc_scr[0], acc_scr[1]
        ot = (acc0[:V_DIM] * (1.0 / acc0[V_DIM:V_DIM + 1])
              - acc1[:V_DIM] * (lam / acc1[V_DIM:V_DIM + 1]))
        r0 = pl.multiple_of(qi * tq, tq)
        o_ref[pl.ds(r0, tq), :] = (_rms(ot.T, g_ref[...]) * (1.0 - lam_init)).astype(o_ref.dtype)
        return carry

    prep_q(0, 0)
    scores(0, 0, 0)
    lax.fori_loop(0, nq, query_tile, 0)


def _attention(z, lamp, subln_g, *, batch, seq, lam_init, tq, tk):
    T = z.shape[0]
    tq, tk = min(tq, seq), min(tk, seq // 2)
    group = min(KEY_CHUNKS_PER_ITER, seq // tk)
    assert group % 2 == 0 and seq % (group * tk) == 0 and seq % tq == 0
    seq_block = lambda col: pl.BlockSpec((seq, V_DIM), lambda b, h: (b, col * N_HEADS + h))
    return pl.pallas_call(
        functools.partial(_attn_kernel, lam_init=lam_init, seq=seq, tq=tq, tk=tk, group=group),
        grid=(batch, N_HEADS),
        in_specs=[
            pl.BlockSpec((4, HEAD_DIM), lambda b, h: (0, 0)),
            pl.BlockSpec((1, V_DIM), lambda b, h: (0, 0)),
            seq_block(COL_Q), seq_block(COL_K), seq_block(COL_V),
        ],
        out_specs=pl.BlockSpec((seq, V_DIM), lambda b, h: (b, h)),
        out_shape=jax.ShapeDtypeStruct((T, ATT_W), BF16),
        scratch_shapes=[pltpu.VMEM((2, 2, V_DIM, tq), BF16),
                        pltpu.VMEM((V_DIM + BF16_ROWS, seq), BF16),
                        pltpu.VMEM((2, V_DIM + BF16_ROWS, tq), F32),
                        pltpu.VMEM((2, 2, tk, tq), F32),
                        pltpu.VMEM((2, 2, 1, tq), F32)],
        compiler_params=_cparams("parallel", "parallel"),
        name="diff_attention",
    )(lamp, subln_g, z, z, z)


def _softplus(y):
    return jnp.maximum(y, 0.0) + jnp.log1p(jnp.exp(-jnp.abs(y)))


def _lru_gates(xc_ref, wri_ref, bri_ref, lam_ref, a_scr, b_scr):
    xcb = xc_ref[...]
    xc = xcb.astype(F32)
    c_softplus = LRU_C * _softplus(-lam_ref[...])
    for n in range(N_LRU_BLOCKS):
        sl = slice(n * LRU_BLOCK, (n + 1) * LRU_BLOCK)
        t = jnp.tanh(jnp.dot(xcb[:, sl], wri_ref[n], preferred_element_type=F32) + bri_ref[n])
        r = 0.5 * t[:, :LRU_BLOCK] + 0.5
        i = 0.5 * t[:, LRU_BLOCK:] + 0.5
        neg_log_a = r * c_softplus[:, sl]
        a = jnp.exp2(neg_log_a * (-LOG2_E))
        a_scr[:, sl] = a
        one_minus_a2 = jnp.tanh(neg_log_a) * (a * a + 1.0)
        root = one_minus_a2 * lax.rsqrt(jnp.maximum(one_minus_a2, TINY))
        b_scr[:, sl] = root * (i * xc[:, sl])


def _lru_scan(a_scr, b_scr, h_scr, *, reverse, ts, unrolled):
    rows = lax.broadcasted_iota(jnp.int32, (SUBLANES, LRU_W), 0)
    ng = ts // SUBLANES

    def group(g, h):
        gi = (ng - 1 - g) if reverse else g
        r0 = gi * SUBLANES if unrolled else pl.multiple_of(gi * SUBLANES, SUBLANES)
        a8 = a_scr[pl.ds(r0, SUBLANES), :]
        b8 = b_scr[pl.ds(r0, SUBLANES), :]
        for d in (1, 2, 4):
            shift = (SUBLANES - d) if reverse else d
            keep = (rows < SUBLANES - d) if reverse else (rows >= d)
            a_sh = jnp.where(keep, pltpu.roll(a8, shift, 0), 1.0)
            b_sh = jnp.where(keep, pltpu.roll(b8, shift, 0), 0.0)
            b8 = a8 * b_sh + b8
            a8 = a8 * a_sh
        h8 = a8 * h + b8
        b_scr[pl.ds(r0, SUBLANES), :] = h8
        return h8[0:1, :] if reverse else h8[SUBLANES - 1:SUBLANES, :]

    if unrolled:
        h = h_scr[...]
        for g in range(ng):
            h = group(g, h)
        h_scr[...] = h
    else:
        h_scr[...] = lax.fori_loop(0, ng, group, h_scr[...])


def _lru_fwd_kernel(xc_ref, wri_ref, bri_ref, lam_ref, o_ref, a_scr, b_scr, h_scr, *, ts):
    @pl.when(pl.program_id(1) == 0)
    def _():
        h_scr[...] = jnp.zeros(h_scr.shape, F32)

    _lru_gates(xc_ref, wri_ref, bri_ref, lam_ref, a_scr, b_scr)
    _lru_scan(a_scr, b_scr, h_scr, reverse=False, ts=ts, unrolled=False)
    o_ref[...] = b_scr[...]


def _lru_bwd_merge_kernel(xc_ref, wri_ref, bri_ref, lam_ref, gl_ref, hf_ref, oa_ref, ga_ref, gb_ref,
                          x_ref, wba_ref, wbl_ref, wo_ref, g_ref, o_ref, a_scr, b_scr, h_scr, *, ts):
    @pl.when(pl.program_id(1) == 0)
    def _():
        h_scr[...] = jnp.zeros(h_scr.shape, F32)

    _lru_gates(xc_ref, wri_ref, bri_ref, lam_ref, a_scr, b_scr)
    _lru_scan(a_scr, b_scr, h_scr, reverse=True, ts=ts, unrolled=True)
    o_lru = (gl_ref[...].astype(F32) * (hf_ref[...] + b_scr[...])).astype(BF16)
    pa = jnp.dot(oa_ref[...], wba_ref[...], preferred_element_type=F32)
    pb = jnp.dot(o_lru, wbl_ref[...], preferred_element_type=F32)
    m = ((jnp.tanh(ga_ref[...].astype(F32)) + 1.0) * pa
         + (jnp.tanh(gb_ref[...].astype(F32)) + 1.0) * pb)
    y = jnp.dot(m.astype(BF16), wo_ref[...], preferred_element_type=F32)
    o_ref[...] = x_ref[...] + _rms(y, g_ref[...])


def _lru_fwd(z, wri, bri, lam, *, batch, seq, ts):
    T = z.shape[0]
    ts = min(ts, seq)
    ns = seq // ts
    return pl.pallas_call(
        functools.partial(_lru_fwd_kernel, ts=ts),
        grid=(batch, ns),
        in_specs=[
            pl.BlockSpec((ts, LRU_W), lambda b, j: (b * ns + j, COL_U)),
            pl.BlockSpec(wri.shape, lambda b, j: (0, 0, 0)),
            pl.BlockSpec(bri.shape, lambda b, j: (0, 0, 0)),
            pl.BlockSpec((1, LRU_W), lambda b, j: (0, 0)),
        ],
        out_specs=pl.BlockSpec((ts, LRU_W), lambda b, j: (b * ns + j, 0)),
        out_shape=jax.ShapeDtypeStruct((T, LRU_W), F32),
        scratch_shapes=[pltpu.VMEM((ts, LRU_W), F32), pltpu.VMEM((ts, LRU_W), F32),
                        pltpu.VMEM((1, LRU_W), F32)],
        compiler_params=_cparams("parallel", "arbitrary"),
        name="lru_fwd",
    )(z, wri, bri, lam)


def _lru_bwd_merge(z, wri, bri, lam, h_fwd, o_attn, x, w_ba, w_bl, w_out, g, *, batch, seq, ts):
    T = z.shape[0]
    ts = min(ts, seq)
    ns = seq // ts
    tile = lambda col: pl.BlockSpec((ts, D_MODEL), lambda b, j: (b * ns + ns - 1 - j, col))
    const2 = lambda b, j: (0, 0)
    const3 = lambda b, j: (0, 0, 0)
    wspec = pl.BlockSpec((D_MODEL, D_MODEL), const2, pipeline_mode=pl.Buffered(1))
    return pl.pallas_call(
        functools.partial(_lru_bwd_merge_kernel, ts=ts),
        grid=(batch, ns),
        in_specs=[
            tile(COL_U),
            pl.BlockSpec(wri.shape, const3),
            pl.BlockSpec(bri.shape, const3),
            pl.BlockSpec((1, LRU_W), const2),
            tile(COL_GL), tile(0), tile(0), tile(COL_GA), tile(COL_GB), tile(0),
            wspec, wspec, wspec,
            pl.BlockSpec((1, D_MODEL), const2),
        ],
        out_specs=tile(0),
        out_shape=jax.ShapeDtypeStruct((T, D_MODEL), F32),
        scratch_shapes=[pltpu.VMEM((ts, LRU_W), F32), pltpu.VMEM((ts, LRU_W), F32),
                        pltpu.VMEM((1, LRU_W), F32)],
        compiler_params=_cparams("parallel", "arbitrary"),
        name="lru_bwd_merge",
    )(z, wri, bri, lam, z, h_fwd, o_attn, z, z, x, w_ba, w_bl, w_out, g)


def _rope_tables(seq):
    half = HEAD_DIM // 2
    inv = ROPE_THETA ** (-jnp.arange(0, HEAD_DIM, 2, dtype=F32) / HEAD_DIM)
    ang = jnp.arange(seq, dtype=F32)[:, None] * inv[None, :]
    lane = jnp.arange(LANES)
    cos = jnp.cos(ang)[:, lane % half]
    sin = jnp.sin(ang)[:, lane % half]
    lower = (lane % HEAD_DIM) < half
    return cos, jnp.where(lower, -sin, 0.0), jnp.where(lower, 0.0, sin)


def _prep_layer(i, w):
    col = jnp.arange(IN_COLS)
    q_scale = jnp.where(col < ATT_W, HEAD_DIM ** -0.5,
                        jnp.where(col >= COL_GA * D_MODEL, 0.5, 1.0)).astype(F32)
    row = lambda a: a.reshape(1, -1).astype(F32)
    wri = (0.5 * jnp.concatenate([w["rg_wa"][i], w["rg_wx"][i]], axis=-1)).astype(BF16)
    bri = 0.5 * jnp.concatenate([w["rg_ba"][i].reshape(2, N_LRU_BLOCKS, 1, LRU_BLOCK),
                                 w["rg_bx"][i].reshape(2, N_LRU_BLOCKS, 1, LRU_BLOCK)], axis=-1)
    return dict(
        norm_mix_pre=row(w["norm_mix_pre"][i]),
        norm_mix_post=row(w["norm_mix_post"][i]),
        w_in=(w["w_in"][i] * q_scale).astype(BF16),
        lamp=jnp.stack([w["lam_q1"][i], w["lam_k1"][i], w["lam_q2"][i], w["lam_k2"][i]]).astype(F32),
        subln_g=row(w["subln_g"][i]),
        lru_conv_w=w["lru_conv_w"][i].astype(F32),
        lru_conv_b=row(w["lru_conv_b"][i]),
        wri=wri, bri=bri.astype(F32),
        rg_lambda=w["rg_lambda"][i].reshape(2, 1, LRU_W).astype(F32),
        w_branch_attn=w["w_branch_attn"][i].astype(BF16),
        w_branch_lru=w["w_branch_lru"][i].astype(BF16),
        w_out=(0.5 * w["w_out"][i]).astype(BF16),
        norm_ffn_pre=row(w["norm_ffn_pre"][i]),
        norm_ffn_post=row(w["norm_ffn_post"][i]),
        w_ffn_up=w["w_ffn_up"][i].astype(BF16),
        ffn_conv_w=w["ffn_conv_w"][i].astype(F32),
        ffn_conv_b=row(w["ffn_conv_b"][i]),
        w_ffn_down=(0.5 * w["w_ffn_down"][i]).astype(BF16),
        w_ple_proj=(0.5 * w["w_ple_proj"][i]).astype(BF16),
        w_ple_gate=(0.5 * w["w_ple_gate"][i]).astype(BF16),
    )


def _layer(x, p, lp, rope, lam_init, *, batch, seq, tiles):
    z = _in_proj(x, lp["norm_mix_pre"], lp["w_in"], rope, lp["lru_conv_w"], lp["lru_conv_b"],
                 seq=seq, tm=tiles["tm"], tn=tiles["tn"])
    o_attn = _attention(z, lp["lamp"], lp["subln_g"], batch=batch, seq=seq, lam_init=lam_init,
                        tq=tiles["tq"], tk=tiles["tk"])
    h_fwd = _lru_fwd(z, lp["wri"][0], lp["bri"][0], lp["rg_lambda"][0],
                     batch=batch, seq=seq, ts=tiles["ts"])
    x1 = _lru_bwd_merge(z, lp["wri"][1], lp["bri"][1], lp["rg_lambda"][1], h_fwd, o_attn, x,
                        lp["w_branch_attn"], lp["w_branch_lru"], lp["w_out"], lp["norm_mix_post"],
                        batch=batch, seq=seq, ts=tiles["ts_merge"])
    return _ffn(x1, p, lp["norm_ffn_pre"], lp["w_ffn_up"], lp["ffn_conv_w"], lp["ffn_conv_b"],
                lp["w_ffn_down"], lp["norm_ffn_post"], lp["w_ple_gate"], lp["w_ple_proj"],
                seq=seq, tm=tiles["tm_ffn"], tn=tiles["tn"])


TILES = dict(tm=512, tn=1024, tq=512, tk=512, ts=1024, ts_merge=512, tm_ffn=512)


def _trunk(x, p, layers, tiles):
    batch, seq, _ = x.shape
    rope = _rope_tables(seq)
    h = x.reshape(batch * seq, D_MODEL)
    for i, lp in enumerate(layers):
        lam_init = 0.8 - 0.6 * math.exp(-0.3 * i)
        h = _layer(h, p[i].reshape(batch * seq, PLE_DIM), lp, rope, lam_init,
                   batch=batch, seq=seq, tiles=tiles)
    return h.reshape(batch, seq, D_MODEL)


def kernel(x_prompt, x_sample, p_prompt, p_sample, norm_mix_pre, norm_mix_post, w_in, lam_q1, lam_k1, lam_q2, lam_k2, subln_g, lru_conv_w, lru_conv_b, rg_wa, rg_ba, rg_wx, rg_bx, rg_lambda, w_branch_attn, w_branch_lru, w_out, norm_ffn_pre, norm_ffn_post, w_ffn_up, ffn_conv_w, ffn_conv_b, w_ffn_down, w_ple_proj, w_ple_gate):
    w = dict(norm_mix_pre=norm_mix_pre, norm_mix_post=norm_mix_post, w_in=w_in, lam_q1=lam_q1,
             lam_k1=lam_k1, lam_q2=lam_q2, lam_k2=lam_k2, subln_g=subln_g, lru_conv_w=lru_conv_w,
             lru_conv_b=lru_conv_b, rg_wa=rg_wa, rg_ba=rg_ba, rg_wx=rg_wx, rg_bx=rg_bx,
             rg_lambda=rg_lambda, w_branch_attn=w_branch_attn, w_branch_lru=w_branch_lru, w_out=w_out,
             norm_ffn_pre=norm_ffn_pre, norm_ffn_post=norm_ffn_post, w_ffn_up=w_ffn_up,
             ffn_conv_w=ffn_conv_w, ffn_conv_b=ffn_conv_b, w_ffn_down=w_ffn_down,
             w_ple_proj=w_ple_proj, w_ple_gate=w_ple_gate)
    layers = [_prep_layer(i, w) for i in range(w_in.shape[0])]
    return (_trunk(x_prompt, p_prompt, layers, TILES), _trunk(x_sample, p_sample, layers, TILES))
```
